```python
import jax, jax.numpy as jnp
from jax import lax
import numpy as np

D_MODEL = 1024
BATCH = 16
SEQ = 2048
DEPTH = 4

NSA_HEADS = 8
NSA_KV_GROUPS = 2
NSA_HPG = NSA_HEADS // NSA_KV_GROUPS
NSA_HEAD_DIM = 64
NSA_WIDTH = NSA_HEADS * NSA_HEAD_DIM
NSA_KV_WIDTH = NSA_KV_GROUPS * NSA_HEAD_DIM
NSA_N_KV = 6
CMP_BLOCK = 32
CMP_STRIDE = 16
SEL_BLOCK = 64
SEL_TOP_N = 16
WINDOW = 512
NSA_QBLOCK = 64
POOL_WINDOWS = (2, 4, 8, 16)
POOL_GROUPS = len(POOL_WINDOWS)
POOL_WIDTH = D_MODEL - NSA_WIDTH
POOL_GROUP_DIM = POOL_WIDTH // POOL_GROUPS
A_IN = NSA_WIDTH + NSA_N_KV * NSA_KV_WIDTH + 3 * NSA_HEADS + POOL_WIDTH
GLA_HEADS = 4
GLA_KEY_WIDTH = D_MODEL // 2
GLA_VAL_WIDTH = D_MODEL
GLA_DK = GLA_KEY_WIDTH // GLA_HEADS
GLA_DV = GLA_VAL_WIDTH // GLA_HEADS
GLA_GATE_RANK = 16
GLA_TAU = 16.0
GLA_CHUNK = 64
C_IN = 2 * GLA_KEY_WIDTH + 2 * GLA_VAL_WIDTH + GLA_GATE_RANK
D_FF = 4 * D_MODEL
N_EVEN = (DEPTH + 1) // 2
N_ODD = DEPTH // 2
ALPHA = (2 * DEPTH) ** 0.25
BETA = (8 * DEPTH) ** -0.25
LN_EPS = 1e-5
MASK_VALUE = -1e30
FORCE_SCORE = 1e4

kernel_name = "nsa_pool_gla_deepnorm_hybrid"


def layer_norm(x, g, b):
    xf = x.astype(jnp.float32)
    mu = jnp.mean(xf, axis=-1, keepdims=True)
    var = jnp.mean(jnp.square(xf - mu), axis=-1, keepdims=True)
    return ((xf - mu) * lax.rsqrt(var + LN_EPS) * g + b).astype(x.dtype)


def masked_softmax(s, mask):
    s = jnp.where(mask, s.astype(jnp.float32), MASK_VALUE)
    m = jnp.max(s, axis=-1, keepdims=True)
    e = jnp.where(mask, jnp.exp(s - m), 0.0)
    return e / jnp.maximum(jnp.sum(e, axis=-1, keepdims=True), 1e-30)


def compress_blocks(k, pos, w1, b1, w2):
    S = k.shape[2]
    n_cmp = (S - CMP_BLOCK) // CMP_STRIDE + 1
    idx = np.arange(n_cmp)[:, None] * CMP_STRIDE + np.arange(CMP_BLOCK)[None, :]
    blocks = k[:, :, idx, :] + pos
    flat = blocks.reshape(blocks.shape[:3] + (CMP_BLOCK * NSA_HEAD_DIM,))
    return jax.nn.gelu(flat @ w1 + b1) @ w2


def selection_map(n_cmp, n_sel):
    sub = np.arange(n_cmp)[:, None] + np.arange(CMP_BLOCK // CMP_STRIDE)[None, :]
    owner = sub // (SEL_BLOCK // CMP_STRIDE)
    return (owner[:, :, None] == np.arange(n_sel)[None, None, :]).sum(1).astype(np.float32)


def pool_mixer(u, pool_w, pool_scale):
    B, S, _ = u.shape
    uf = u.astype(jnp.float32).reshape(B, S, POOL_GROUPS, POOL_GROUP_DIM)
    cs = jnp.pad(jnp.cumsum(uf, axis=1), ((0, 0), (1, 0), (0, 0), (0, 0)))
    count_base = jnp.arange(1, S + 1, dtype=jnp.float32)
    outs = []
    for g, w in enumerate(POOL_WINDOWS):
        c = cs[:, :, g]
        lower = jnp.pad(c[:, :S + 1 - w], ((0, 0), (w - 1, 0), (0, 0)))
        mean = (c[:, 1:] - lower) / jnp.minimum(count_base, float(w))[None, :, None]
        outs.append(mean - uf[:, :, g])
    r = jnp.stack(outs, axis=2)
    y = jnp.einsum('bsgc,gcd->bsgd', r, pool_w.astype(jnp.float32))
    return (y.reshape(B, S, POOL_WIDTH) * pool_scale).astype(u.dtype)


def nsa_pool_mixer(x, w_in, cmp_pos, cmp_w1, cmp_b1, cmp_w2, pool_w, pool_scale, w_out):
    B, S, _ = x.shape
    G, HPG, DH = NSA_KV_GROUPS, NSA_HPG, NSA_HEAD_DIM
    h = x @ w_in
    o1 = NSA_WIDTH
    o2 = o1 + NSA_N_KV * NSA_KV_WIDTH
    o3 = o2 + 3 * NSA_HEADS
    q = h[..., :o1].reshape(B, S, G, HPG, DH).transpose(0, 2, 3, 1, 4)
    kv = h[..., o1:o2].reshape(B, S, NSA_N_KV, G, DH).transpose(2, 0, 3, 1, 4)
    gates = jax.nn.sigmoid(h[..., o2:o3].astype(jnp.float32))
    gates = gates.reshape(B, S, G, HPG, 3).transpose(0, 2, 3, 1, 4)
    u = h[..., o3:]
    k_cmp, v_cmp, k_slc, v_slc, k_win, v_win = kv[0], kv[1], kv[2], kv[3], kv[4], kv[5]

    kc = compress_blocks(k_cmp, cmp_pos[0], cmp_w1[0], cmp_b1[0], cmp_w2[0])
    vc = compress_blocks(v_cmp, cmp_pos[1], cmp_w1[1], cmp_b1[1], cmp_w2[1])
    n_cmp = kc.shape[2]
    n_sel = S // SEL_BLOCK
    top_n = min(SEL_TOP_N, n_sel)
    sel_map = jnp.asarray(selection_map(n_cmp, n_sel))
    cmp_end = jnp.arange(n_cmp) * CMP_STRIDE + CMP_BLOCK - 1
    ks_b = k_slc.reshape(B, G, n_sel, SEL_BLOCK, DH)
    vs_b = v_slc.reshape(B, G, n_sel, SEL_BLOCK, DH)
    kw = jnp.pad(k_win, ((0, 0), (0, 0), (WINDOW, 0), (0, 0)))
    vw = jnp.pad(v_win, ((0, 0), (0, 0), (WINDOW, 0), (0, 0)))
    bi = jnp.arange(B)[:, None, None, None]
    gi = jnp.arange(G)[None, :, None, None]
    scale = DH ** -0.5

    def query_block(i):
        t0 = i * NSA_QBLOCK
        t = t0 + jnp.arange(NSA_QBLOCK)
        qb = lax.dynamic_slice_in_dim(q, t0, NSA_QBLOCK, axis=3) * scale
        gb = lax.dynamic_slice_in_dim(gates, t0, NSA_QBLOCK, axis=3)
        s_c = jnp.einsum('bghqd,bgcd->bghqc', qb, kc)
        p_c = masked_softmax(s_c, cmp_end[None, :] <= t[:, None])
        o_c = jnp.einsum('bghqc,bgcd->bghqd', p_c.astype(vc.dtype), vc)
        imp = jnp.einsum('bghqc,cj->bgqj', p_c, sel_map)
        blk = jnp.arange(n_sel)[None, :]
        cur = (t // SEL_BLOCK)[:, None]
        forced = (blk == 0) | (blk == cur) | (blk == cur - 1)
        imp = jnp.where(forced, FORCE_SCORE, jnp.where(blk > cur, -FORCE_SCORE, imp))
        _, sel = lax.top_k(imp, top_n)
        k_sel = ks_b[bi, gi, sel].reshape(B, G, NSA_QBLOCK, top_n * SEL_BLOCK, DH)
        v_sel = vs_b[bi, gi, sel].reshape(B, G, NSA_QBLOCK, top_n * SEL_BLOCK, DH)
        kpos = (sel[..., None] * SEL_BLOCK + jnp.arange(SEL_BLOCK)).reshape(
            B, G, NSA_QBLOCK, top_n * SEL_BLOCK)
        s_s = jnp.einsum('bghqd,bgqnd->bghqn', qb, k_sel)
        p_s = masked_softmax(s_s, (kpos <= t[:, None])[:, :, None])
        o_s = jnp.einsum('bghqn,bgqnd->bghqd', p_s.astype(v_sel.dtype), v_sel)
        k_w = lax.dynamic_slice_in_dim(kw, t0, WINDOW + NSA_QBLOCK, axis=2)
        v_w = lax.dynamic_slice_in_dim(vw, t0, WINDOW + NSA_QBLOCK, axis=2)
        wpos = t0 - WINDOW + jnp.arange(WINDOW + NSA_QBLOCK)
        wmask = ((wpos[None, :] <= t[:, None]) & (wpos[None, :] > t[:, None] - WINDOW)
                 & (wpos[None, :] >= 0))
        s_w = jnp.einsum('bghqd,bgkd->bghqk', qb, k_w)
        p_w = masked_softmax(s_w, wmask)
        o_w = jnp.einsum('bghqk,bgkd->bghqd', p_w.astype(v_w.dtype), v_w)
        o = gb[..., 0:1] * o_c + gb[..., 1:2] * o_s + gb[..., 2:3] * o_w
        return o.astype(x.dtype)

    ob = lax.map(query_block, jnp.arange(S // NSA_QBLOCK))
    o_nsa = ob.transpose(1, 0, 4, 2, 3, 5).reshape(B, S, NSA_WIDTH)
    o_pool = pool_mixer(u, pool_w, pool_scale)
    return jnp.concatenate([o_nsa, o_pool], axis=-1) @ w_out


def gla_mixer(x, w_in, gate_w2, gate_b, norm_g, w_out):
    B, S, _ = x.shape
    N, C, H = S // GLA_CHUNK, GLA_CHUNK, GLA_HEADS
    h = x @ w_in
    o1 = GLA_KEY_WIDTH
    o2 = o1 + GLA_KEY_WIDTH
    o3 = o2 + GLA_VAL_WIDTH
    o4 = o3 + GLA_VAL_WIDTH
    q_, k_, v_, r, a = h[..., :o1], h[..., o1:o2], h[..., o2:o3], h[..., o3:o4], h[..., o4:]
    loga = jax.nn.log_sigmoid((a @ gate_w2 + gate_b).astype(jnp.float32)) / GLA_TAU

    def heads(t, d):
        return t.reshape(B, N, C, H, d).transpose(0, 3, 1, 2, 4).astype(jnp.float32)

    q = heads(q_, GLA_DK) * GLA_DK ** -0.5
    k = heads(k_, GLA_DK)
    v = heads(v_, GLA_DV)
    b = jnp.cumsum(heads(loga, GLA_DK), axis=3)
    q_t = q * jnp.exp(b)
    k_t = k * jnp.exp(-b)
    causal = jnp.tril(jnp.ones((C, C), dtype=bool))
    att = jnp.where(causal, jnp.einsum('bhncd,bhnjd->bhncj', q_t, k_t), 0.0)
    o_intra = jnp.einsum('bhncj,bhnje->bhnce', att, v)
    b_last = b[:, :, :, -1:, :]
    upd = jnp.einsum('bhncd,bhnce->bhnde', k * jnp.exp(b_last - b), v)
    decay = jnp.exp(b_last[:, :, :, 0])

    def step(state, inp):
        d, u = inp
        return d[..., None] * state + u, state

    s0 = jnp.zeros((B, H, GLA_DK, GLA_DV), jnp.float32)
    _, s_prev = lax.scan(step, s0, (jnp.moveaxis(decay, 2, 0), jnp.moveaxis(upd, 2, 0)))
    s_prev = jnp.moveaxis(s_prev, 0, 2)
    o = o_intra + jnp.einsum('bhncd,bhnde->bhnce', q_t, s_prev)
    o = o * lax.rsqrt(jnp.mean(jnp.square(o), axis=-1, keepdims=True) + LN_EPS) * norm_g
    o = o.transpose(0, 2, 3, 1, 4).reshape(B, S, GLA_VAL_WIDTH).astype(x.dtype)
    return (o * jax.nn.silu(r)) @ w_out


def sqrelu_mlp(x, w1, w2):
    return jnp.square(jax.nn.relu(x @ w1)) @ w2


def setup_inputs(seed: int = 0) -> dict:
    key = jax.random.key(seed)
    ks = jax.random.split(key, 24)

    def nrm(k, shape, scale):
        return jax.random.normal(k, shape, jnp.float32) * scale

    DH = NSA_HEAD_DIM
    return {
        "x": nrm(ks[0], (BATCH, SEQ, D_MODEL), 1.0),
        "a_w_in": nrm(ks[1], (N_EVEN, D_MODEL, A_IN), D_MODEL ** -0.5),
        "a_cmp_pos": nrm(ks[2], (N_EVEN, 2, CMP_BLOCK, DH), 0.1),
        "a_cmp_w1": nrm(ks[3], (N_EVEN, 2, CMP_BLOCK * DH, DH), (CMP_BLOCK * DH) ** -0.5),
        "a_cmp_b1": nrm(ks[4], (N_EVEN, 2, DH), 0.02),
        "a_cmp_w2": nrm(ks[5], (N_EVEN, 2, DH, DH), DH ** -0.5),
        "a_pool_w": nrm(ks[6], (N_EVEN, POOL_GROUPS, POOL_GROUP_DIM, POOL_GROUP_DIM), POOL_GROUP_DIM ** -0.5),
        "a_pool_scale": 1.0 + nrm(ks[7], (N_EVEN, POOL_WIDTH), 0.1),
        "a_w_out": nrm(ks[8], (N_EVEN, D_MODEL, D_MODEL), BETA * D_MODEL ** -0.5),
        "c_w_in": nrm(ks[9], (N_ODD, D_MODEL, C_IN), D_MODEL ** -0.5),
        "c_gate_w2": nrm(ks[10], (N_ODD, GLA_GATE_RANK, GLA_KEY_WIDTH), GLA_GATE_RANK ** -0.5),
        "c_gate_b": nrm(ks[11], (N_ODD, GLA_KEY_WIDTH), 0.1),
        "c_norm_g": 1.0 + nrm(ks[12], (N_ODD, GLA_DV), 0.1),
        "c_w_out": nrm(ks[13], (N_ODD, D_MODEL, D_MODEL), BETA * D_MODEL ** -0.5),
        "ln1_g": 1.0 + nrm(ks[14], (DEPTH, D_MODEL), 0.05),
        "ln1_b": nrm(ks[15], (DEPTH, D_MODEL), 0.02),
        "ln2_g": 1.0 + nrm(ks[16], (DEPTH, D_MODEL), 0.05),
        "ln2_b": nrm(ks[17], (DEPTH, D_MODEL), 0.02),
        "mlp_w1": nrm(ks[18], (DEPTH, D_MODEL, D_FF), D_MODEL ** -0.5),
        "mlp_w2": nrm(ks[19], (DEPTH, D_FF, D_MODEL), BETA * D_FF ** -0.5),
    }


def reference(x, a_w_in, a_cmp_pos, a_cmp_w1, a_cmp_b1, a_cmp_w2, a_pool_w, a_pool_scale,
              a_w_out, c_w_in, c_gate_w2, c_gate_b, c_norm_g, c_w_out,
              ln1_g, ln1_b, ln2_g, ln2_b, mlp_w1, mlp_w2):
    for i in range(DEPTH):
        j = i // 2
        if i % 2 == 0:
            mix = nsa_pool_mixer(x, a_w_in[j], a_cmp_pos[j], a_cmp_w1[j], a_cmp_b1[j],
                                 a_cmp_w2[j], a_pool_w[j], a_pool_scale[j], a_w_out[j])
        else:
            mix = gla_mixer(x, c_w_in[j], c_gate_w2[j], c_gate_b[j], c_norm_g[j], c_w_out[j])
        x = layer_norm(ALPHA * x + mix, ln1_g[i], ln1_b[i])
        x = layer_norm(ALPHA * x + sqrelu_mlp(x, mlp_w1[i], mlp_w2[i]), ln2_g[i], ln2_b[i])
    return x
```

```python
import functools

import numpy as np
import jax
import jax.numpy as jnp
from jax import lax
from jax.experimental import pallas as pl
from jax.experimental.pallas import tpu as pltpu

F32 = jnp.float32
BF16 = jnp.bfloat16

D_MODEL = 1024
DEPTH = 4
NSA_HEADS = 8
NSA_GROUPS = 2
NSA_HPG = NSA_HEADS // NSA_GROUPS
NSA_DH = 64
NSA_WIDTH = NSA_HEADS * NSA_DH
NSA_KV_WIDTH = NSA_GROUPS * NSA_DH
CMP_BLOCK = 32
CMP_STRIDE = 16
SEL_BLOCK = 64
SEL_TOP_N = 16
WINDOW = 512
POOL_WINDOWS = (2, 4, 8, 16)
POOL_GROUP_DIM = 128
POOL_WIDTH = 512
GLA_HEADS = 4
GLA_DK = 128
GLA_DV = 256
GLA_KEY_WIDTH = GLA_HEADS * GLA_DK
GLA_VAL_WIDTH = GLA_HEADS * GLA_DV
GLA_RANK = 16
GLA_TAU = 16.0
GLA_CHUNK = 64
D_FF = 4 * D_MODEL
ALPHA = (2 * DEPTH) ** 0.25
LN_EPS = 1e-5
MASK_VALUE = -1e30
FORCE_SCORE = 1e4

LANES = 128
VMEM_LIMIT_BYTES = 56 * 1024 * 1024

PROJ_TM = 512
ATT_TQ = 128
ATT_CK = 512
FF_CHUNK = 512
POOL_HALO = 16


def _dot(a, b):
    return jnp.dot(a, b, preferred_element_type=F32)


def _dot_nt(a, b):
    return lax.dot_general(a, b, (((1,), (1,)), ((), ())), preferred_element_type=F32)


def _dot_tn(a, b):
    return lax.dot_general(a, b, (((0,), (0,)), ((), ())), preferred_element_type=F32)


def _params(*semantics):
    return pltpu.CompilerParams(dimension_semantics=semantics, vmem_limit_bytes=VMEM_LIMIT_BYTES)


def _layer_norm(z, g, b):
    mu = jnp.mean(z, axis=-1, keepdims=True)
    zc = z - mu
    var = jnp.mean(zc * zc, axis=-1, keepdims=True)
    return zc * lax.rsqrt(var + LN_EPS) * g + b


def _proj_even_kernel(x_ref, wq_ref, wkv_ref, wg_ref, wu_ref, eb_ref,
                      q_ref, kc_ref, vc_ref, ks_ref, vs_ref, kw_ref, vw_ref, g_ref, u_ref):
    xb = x_ref[0].astype(BF16)
    for h in range(NSA_HEADS):
        q_ref[0, h] = _dot(xb, wq_ref[:, h * LANES:(h + 1) * LANES]).astype(BF16)
    kv = [_dot(xb, wkv_ref[:, i * LANES:(i + 1) * LANES]) for i in range(6)]
    kc_ref[0] = kv[0]
    vc_ref[0] = kv[1]
    ks_ref[0, :, :LANES] = kv[2].astype(BF16)
    ks_ref[0, :, LANES:] = eb_ref[...]
    vs_ref[0] = kv[3].astype(BF16)
    kw_ref[0] = kv[4].astype(BF16)
    vw_ref[0] = kv[5].astype(BF16)
    g_ref[0] = jax.nn.sigmoid(_dot(xb, wg_ref[...]))
    for i in range(POOL_WIDTH // LANES):
        u_ref[0, :, i * LANES:(i + 1) * LANES] = _dot(xb, wu_ref[:, i * LANES:(i + 1) * LANES])


def _proj_even(x, wq, wkv, wg, wu, eb):
    B, S, D = x.shape
    tm = PROJ_TM
    row = lambda n: pl.BlockSpec((1, tm, n), lambda b, i: (b, i, 0))
    full = lambda a: pl.BlockSpec(a.shape, lambda b, i: (0,) * a.ndim)
    out_shape = (
        jax.ShapeDtypeStruct((B, NSA_HEADS, S, LANES), BF16),
        jax.ShapeDtypeStruct((B, S, LANES), F32),
        jax.ShapeDtypeStruct((B, S, LANES), F32),
        jax.ShapeDtypeStruct((B, S, 2 * LANES), BF16),
        jax.ShapeDtypeStruct((B, S, LANES), BF16),
        jax.ShapeDtypeStruct((B, S, LANES), BF16),
        jax.ShapeDtypeStruct((B, S, LANES), BF16),
        jax.ShapeDtypeStruct((B, S, LANES), F32),
        jax.ShapeDtypeStruct((B, S, POOL_WIDTH), F32),
    )
    out_specs = (
        pl.BlockSpec((1, NSA_HEADS, tm, LANES), lambda b, i: (b, 0, i, 0)),
        row(LANES), row(LANES), row(2 * LANES), row(LANES), row(LANES), row(LANES), row(LANES),
        row(POOL_WIDTH),
    )
    return pl.pallas_call(
        _proj_even_kernel,
        grid=(B, S // tm),
        in_specs=[row(D), full(wq), full(wkv), full(wg), full(wu),
                  pl.BlockSpec((tm, LANES), lambda b, i: (i, 0))],
        out_specs=out_specs,
        out_shape=out_shape,
        compiler_params=_params("parallel", "parallel"),
        name="proj_even",
    )(x, wq, wkv, wg, wu, eb)


def _gelu_tanh(x):
    return 0.5 * x * (1.0 + jnp.tanh(0.7978845608028654 * (x + 0.044715 * (x * x * x))))


def _compress_kernel(kc_ref, vc_ref, pos_ref, w1_ref, b1_ref, w2_ref, okc_ref, ovc_ref):
    for which, (src, dst) in enumerate(((kc_ref, okc_ref), (vc_ref, ovc_ref))):
        xr = src[0]
        nr = xr.shape[0]
        lo = _dot((xr + pos_ref[which, 0]).astype(BF16), w1_ref[which, 0])
        hi = _dot((xr + pos_ref[which, 1]).astype(BF16), w1_ref[which, 1])
        hi_next = pltpu.roll(hi, nr - 1, axis=0)
        act = _gelu_tanh(lo + hi_next + b1_ref[which])
        out = _dot(act.astype(BF16), w2_ref[which])
        rows = lax.broadcasted_iota(jnp.int32, out.shape, 0)
        dst[0] = jnp.where(rows < nr - 1, out, 0.0).astype(BF16)


def _compress(kc, vc, pos, w1, b1, w2):
    B, NR, W = kc.shape
    blk = pl.BlockSpec((1, NR, W), lambda b: (b, 0, 0))
    full = lambda a: pl.BlockSpec(a.shape, lambda b: (0,) * a.ndim)
    oblk = pl.BlockSpec((1, NR, LANES), lambda b: (b, 0, 0))
    return pl.pallas_call(
        _compress_kernel,
        grid=(B,),
        in_specs=[blk, blk, full(pos), full(w1), full(b1), full(w2)],
        out_specs=(oblk, oblk),
        out_shape=(jax.ShapeDtypeStruct((B, NR, LANES), BF16),) * 2,
        compiler_params=_params("parallel"),
        name="compress",
    )(kc, vc, pos, w1, b1, w2)


def _masked_softmax(s, mask):
    s = jnp.where(mask, s, MASK_VALUE)
    m = jnp.max(s, axis=-1, keepdims=True)
    e = jnp.where(mask, jnp.exp(s - m), 0.0)
    return e / jnp.maximum(jnp.sum(e, axis=-1, keepdims=True), 1e-30)


def _nsa_kernel(q_ref, kc_ref, vc_ref, ks_ref, vs_ref, kw_ref, vw_ref, g_ref, selt_ref, o_ref,
                m_ref, l_ref, acc_ref, *, seq_len):
    tq = ATT_TQ
    rows = NSA_HPG * tq
    n_sel = seq_len // SEL_BLOCK
    t0 = pl.program_id(1) * tq
    t_col = t0 + lax.broadcasted_iota(jnp.int32, (tq, 1), 0)
    t_rows = jnp.concatenate([t_col] * NSA_HPG, axis=0)
    gates = g_ref[0]
    lane_out = lax.broadcasted_iota(jnp.int32, (tq, LANES), 1)

    for g in range(NSA_GROUPS):
        q = jnp.concatenate([q_ref[0, NSA_HPG * g + h] for h in range(NSA_HPG)], axis=0)

        s_c = _dot_nt(q, kc_ref[0])
        c_end = lax.broadcasted_iota(jnp.int32, s_c.shape, 1) * CMP_STRIDE + (CMP_BLOCK - 1)
        p_c = _masked_softmax(s_c, c_end <= t_rows)
        o_c = _dot(p_c.astype(BF16), vc_ref[0])

        p_sum = p_c[0:tq]
        for h in range(1, NSA_HPG):
            p_sum = p_sum + p_c[h * tq:(h + 1) * tq]
        p_hi = p_sum.astype(BF16)
        p_lo = (p_sum - p_hi.astype(F32)).astype(BF16)
        imp = _dot_nt(selt_ref[...], p_hi) + _dot_nt(selt_ref[...], p_lo)
        blk = lax.broadcasted_iota(jnp.int32, imp.shape, 0)
        cur = (t0 + lax.broadcasted_iota(jnp.int32, imp.shape, 1)) >> 6
        forced = (blk == 0) | (blk == cur) | (blk == cur - 1)
        score = jnp.where(forced, FORCE_SCORE, jnp.where(blk > cur, -FORCE_SCORE, imp))
        beaten = jnp.zeros(imp.shape, F32)
        for i in range(n_sel):
            s_i = score[i:i + 1, :]
            wins = (s_i > score) | ((s_i == score) & (blk > i))
            beaten = beaten + jnp.where(wins, 1.0, 0.0)
        not_sel = jnp.where(beaten < float(SEL_TOP_N), 0.0, 1.0)
        not_sel = jnp.concatenate([not_sel, jnp.zeros((LANES - n_sel, tq), F32)], axis=0)
        not_sel = not_sel.T.astype(BF16)
        q_sel = jnp.concatenate([q, jnp.concatenate([not_sel] * NSA_HPG, axis=0)], axis=1)

        m_ref[...] = jnp.full(m_ref.shape, MASK_VALUE, F32)
        l_ref[...] = jnp.zeros(l_ref.shape, F32)
        acc_ref[...] = jnp.zeros(acc_ref.shape, F32)

        def chunk(c, carry):
            k0 = pl.multiple_of(c * ATT_CK, ATT_CK)
            s = _dot_nt(q_sel, ks_ref[0, pl.ds(k0, ATT_CK), :])
            kpos = k0 + lax.broadcasted_iota(jnp.int32, s.shape, 1)
            s = jnp.where(kpos <= t_rows, s, MASK_VALUE)
            m_prev = m_ref[...]
            m_new = jnp.maximum(m_prev, jnp.max(s, axis=-1, keepdims=True))
            scale = jnp.exp(m_prev - m_new)
            p = jnp.exp(s - m_new)
            l_ref[...] = scale * l_ref[...] + jnp.sum(p, axis=-1, keepdims=True)
            acc_ref[...] = scale * acc_ref[...] + _dot(p.astype(BF16), vs_ref[0, pl.ds(k0, ATT_CK), :])
            m_ref[...] = m_new
            return carry

        n_chunks = (t0 + tq + ATT_CK - 1) // ATT_CK
        lax.fori_loop(0, n_chunks, chunk, 0)
        o_s = acc_ref[...] / jnp.maximum(l_ref[...], 1e-30)

        w0 = pl.multiple_of(jnp.maximum(t0 - WINDOW, 0), tq)
        s_w = _dot_nt(q, kw_ref[0, pl.ds(w0, WINDOW + tq), :])
        wpos = w0 + lax.broadcasted_iota(jnp.int32, s_w.shape, 1)
        p_w = _masked_softmax(s_w, (wpos <= t_rows) & (wpos > t_rows - WINDOW))
        o_w = _dot(p_w.astype(BF16), vw_ref[0, pl.ds(w0, WINDOW + tq), :])

        heads = []
        for h in range(NSA_HPG):
            r = slice(h * tq, (h + 1) * tq)
            gi = (NSA_HPG * g + h) * 3
            heads.append(gates[:, gi:gi + 1] * o_c[r] + gates[:, gi + 1:gi + 2] * o_s[r]
                         + gates[:, gi + 2:gi + 3] * o_w[r])
        for j in range(NSA_HPG // 2):
            a, b = heads[2 * j], heads[2 * j + 1]
            if g == 0:
                pair = jnp.where(lane_out < NSA_DH, a, pltpu.roll(b, NSA_DH, axis=1))
            else:
                pair = jnp.where(lane_out < NSA_DH, pltpu.roll(a, NSA_DH, axis=1), b)
            c0 = (NSA_HPG * g + 2 * j) * NSA_DH
            o_ref[0, :, c0:c0 + LANES] = pair.astype(BF16)


def _nsa_attention(q, kc, vc, ks, vs, kw, vw, gates, selt):
    B, _, S, _ = q.shape
    tq = ATT_TQ
    rows = NSA_HPG * tq
    seq = lambda n: pl.BlockSpec((1, S, n), lambda b, i: (b, 0, 0))
    cmp_spec = pl.BlockSpec((1,) + kc.shape[1:], lambda b, i: (b, 0, 0))
    return pl.pallas_call(
        functools.partial(_nsa_kernel, seq_len=S),
        grid=(B, S // tq),
        in_specs=[pl.BlockSpec((1, NSA_HEADS, tq, LANES), lambda b, i: (b, 0, i, 0)),
                  cmp_spec, cmp_spec, seq(2 * LANES), seq(LANES), seq(LANES), seq(LANES),
                  pl.BlockSpec((1, tq, LANES), lambda b, i: (b, i, 0)),
                  pl.BlockSpec(selt.shape, lambda b, i: (0, 0))],
        out_specs=pl.BlockSpec((1, tq, NSA_WIDTH), lambda b, i: (b, i, 0)),
        out_shape=jax.ShapeDtypeStruct((B, S, NSA_WIDTH), BF16),
        scratch_shapes=[pltpu.VMEM((rows, 1), F32), pltpu.VMEM((rows, 1), F32),
                        pltpu.VMEM((rows, LANES), F32)],
        compiler_params=_params("parallel", "parallel"),
        name="nsa_attention",
    )(q, kc, vc, ks, vs, kw, vw, gates, selt)


def _outproj_even_kernel(o_ref, u_ref, x_ref, wo_ref, pw_ref, ps_ref, g_ref, b_ref, y_ref, ext_ref):
    tm = u_ref.shape[1]
    i = pl.program_id(1)

    @pl.when(i == 0)
    def _():
        ext_ref[0:POOL_HALO, :] = jnp.zeros((POOL_HALO, POOL_WIDTH), F32)

    u = u_ref[0]
    ext_ref[POOL_HALO:, :] = u
    t = (i * tm + lax.broadcasted_iota(jnp.int32, (tm, 1), 0) + 1).astype(F32)
    mix = _dot(o_ref[0], wo_ref[0:NSA_WIDTH, :])
    for gi, w in enumerate(POOL_WINDOWS):
        cols = slice(gi * POOL_GROUP_DIM, (gi + 1) * POOL_GROUP_DIM)
        u_g = u[:, cols]
        acc = u_g
        for d in range(1, w):
            acc = acc + ext_ref[POOL_HALO - d:POOL_HALO - d + tm, cols]
        r = acc / jnp.minimum(t, float(w)) - u_g
        y_g = _dot(r.astype(BF16), pw_ref[gi]) * ps_ref[:, cols]
        mix = mix + _dot(y_g.astype(BF16), wo_ref[NSA_WIDTH + gi * POOL_GROUP_DIM:
                                                  NSA_WIDTH + (gi + 1) * POOL_GROUP_DIM, :])
    ext_ref[0:POOL_HALO, :] = u[tm - POOL_HALO:, :]
    y_ref[0] = _layer_norm(ALPHA * x_ref[0] + mix, g_ref[...], b_ref[...])


def _outproj_even(o, u, x, wo, pw, ps, g, b):
    B, S, D = x.shape
    tm = PROJ_TM
    row = lambda n: pl.BlockSpec((1, tm, n), lambda bb, i: (bb, i, 0))
    full = lambda a: pl.BlockSpec(a.shape, lambda bb, i: (0,) * a.ndim)
    return pl.pallas_call(
        _outproj_even_kernel,
        grid=(B, S // tm),
        in_specs=[row(NSA_WIDTH), row(POOL_WIDTH), row(D), full(wo), full(pw), full(ps), full(g), full(b)],
        out_specs=row(D),
        out_shape=jax.ShapeDtypeStruct((B, S, D), F32),
        scratch_shapes=[pltpu.VMEM((POOL_HALO + tm, POOL_WIDTH), F32)],
        compiler_params=_params("parallel", "arbitrary"),
        name="outproj_even",
    )(o, u, x, wo, pw, ps, g, b)


def _proj_odd_kernel(x_ref, wq_ref, wk_ref, wv_ref, wr_ref, wa_ref, gw_ref, gb_ref,
                     q_ref, k_ref, v_ref, r_ref, la_ref):
    xb = x_ref[0].astype(BF16)
    n = FF_CHUNK
    for c in range(GLA_KEY_WIDTH // n):
        q_ref[0, :, c * n:(c + 1) * n] = _dot(xb, wq_ref[:, c * n:(c + 1) * n])
        k_ref[0, :, c * n:(c + 1) * n] = _dot(xb, wk_ref[:, c * n:(c + 1) * n])
    for c in range(GLA_VAL_WIDTH // n):
        v_ref[0, :, c * n:(c + 1) * n] = _dot(xb, wv_ref[:, c * n:(c + 1) * n]).astype(BF16)
        r_ref[0, :, c * n:(c + 1) * n] = _dot(xb, wr_ref[:, c * n:(c + 1) * n])
    a = _dot(xb, wa_ref[...])
    z = _dot(a.astype(BF16), gw_ref[...]) + gb_ref[...]
    log_sig = jnp.minimum(z, 0.0) - jnp.log(1.0 + jnp.exp(-jnp.abs(z)))
    la_ref[0] = log_sig / GLA_TAU


def _proj_odd(x, wq, wk, wv, wr, wa, gw, gb):
    B, S, D = x.shape
    tm = PROJ_TM
    row = lambda n: pl.BlockSpec((1, tm, n), lambda b, i: (b, i, 0))
    full = lambda a: pl.BlockSpec(a.shape, lambda b, i: (0,) * a.ndim)
    return pl.pallas_call(
        _proj_odd_kernel,
        grid=(B, S // tm),
        in_specs=[row(D), full(wq), full(wk), full(wv), full(wr), full(wa), full(gw), full(gb)],
        out_specs=(row(GLA_KEY_WIDTH), row(GLA_KEY_WIDTH), row(GLA_VAL_WIDTH), row(GLA_VAL_WIDTH),
                   row(GLA_KEY_WIDTH)),
        out_shape=(jax.ShapeDtypeStruct((B, S, GLA_KEY_WIDTH), F32),
                   jax.ShapeDtypeStruct((B, S, GLA_KEY_WIDTH), F32),
                   jax.ShapeDtypeStruct((B, S, GLA_VAL_WIDTH), BF16),
                   jax.ShapeDtypeStruct((B, S, GLA_VAL_WIDTH), F32),
                   jax.ShapeDtypeStruct((B, S, GLA_KEY_WIDTH), F32)),
        compiler_params=_params("parallel", "parallel"),
        name="proj_odd",
    )(x, wq, wk, wv, wr, wa, gw, gb)


def _gla_kernel(q_ref, k_ref, la_ref, v_ref, r_ref, ng_ref, o_ref, state_ref):
    C = GLA_CHUNK
    n_chunks = q_ref.shape[1] // C
    state_ref[...] = jnp.zeros(state_ref.shape, F32)
    row = lax.broadcasted_iota(jnp.int32, (C, GLA_DK), 0)
    causal = (lax.broadcasted_iota(jnp.int32, (C, C), 0) >= lax.broadcasted_iota(jnp.int32, (C, C), 1))

    def chunk(n, carry):
        r0 = pl.multiple_of(n * C, C)
        rows = pl.ds(r0, C)
        b = la_ref[0, rows, :]
        shift = 1
        while shift < C:
            b = b + jnp.where(row >= shift, pltpu.roll(b, shift, axis=0), 0.0)
            shift *= 2
        b_last = b[C - 1:C, :]
        q_t = (q_ref[0, rows, :] * (GLA_DK ** -0.5)) * jnp.exp(b)
        k = k_ref[0, rows, :]
        k_t = k * jnp.exp(-b)
        k_u = k * jnp.exp(b_last - b)
        v = v_ref[0, rows, :]
        q_tb = q_t.astype(BF16)
        att = jnp.where(causal, _dot_nt(q_tb, k_t.astype(BF16)), 0.0)
        state = state_ref[...]
        o = _dot(att.astype(BF16), v) + _dot_nt(q_tb, state.astype(BF16))
        upd = _dot_tn(v, k_u.astype(BF16))
        state_ref[...] = jnp.exp(b_last) * state + upd
        o = o * lax.rsqrt(jnp.mean(o * o, axis=-1, keepdims=True) + LN_EPS) * ng_ref[...]
        o_ref[0, rows, :] = (o * jax.nn.silu(r_ref[0, rows, :])).astype(BF16)
        return carry

    lax.fori_loop(0, n_chunks, chunk, 0)


def _gla(q, k, la, v, r, ng):
    B, S, _ = q.shape
    key = pl.BlockSpec((1, S, GLA_DK), lambda b, h: (b, 0, h))
    val = pl.BlockSpec((1, S, GLA_DV), lambda b, h: (b, 0, h))
    return pl.pallas_call(
        _gla_kernel,
        grid=(B, GLA_HEADS),
        in_specs=[key, key, key, val, val, pl.BlockSpec(ng.shape, lambda b, h: (0, 0))],
        out_specs=val,
        out_shape=jax.ShapeDtypeStruct((B, S, GLA_VAL_WIDTH), BF16),
        scratch_shapes=[pltpu.VMEM((GLA_DV, GLA_DK), F32)],
        compiler_params=_params("parallel", "parallel"),
        name="gla",
    )(q, k, la, v, r, ng)


def _outproj_odd_kernel(o_ref, x_ref, wo_ref, g_ref, b_ref, y_ref):
    mix = _dot(o_ref[0], wo_ref[...])
    y_ref[0] = _layer_norm(ALPHA * x_ref[0] + mix, g_ref[...], b_ref[...])


def _outproj_odd(o, x, wo, g, b):
    B, S, D = x.shape
    tm = PROJ_TM
    row = lambda n: pl.BlockSpec((1, tm, n), lambda bb, i: (bb, i, 0))
    full = lambda a: pl.BlockSpec(a.shape, lambda bb, i: (0,) * a.ndim)
    return pl.pallas_call(
        _outproj_odd_kernel,
        grid=(B, S // tm),
        in_specs=[row(GLA_VAL_WIDTH), row(D), full(wo), full(g), full(b)],
        out_specs=row(D),
        out_shape=jax.ShapeDtypeStruct((B, S, D), F32),
        compiler_params=_params("parallel", "parallel"),
        name="outproj_odd",
    )(o, x, wo, g, b)


def _mlp_kernel(x_ref, w1_ref, w2_ref, g_ref, b_ref, y_ref, acc_ref):
    x = x_ref[0]
    xb = x.astype(BF16)
    for c in range(D_FF // FF_CHUNK):
        cols = slice(c * FF_CHUNK, (c + 1) * FF_CHUNK)
        h = jnp.maximum(_dot(xb, w1_ref[:, cols]), 0.0)
        part = _dot((h * h).astype(BF16), w2_ref[cols, :])
        if c == 0:
            acc_ref[...] = part
        else:
            acc_ref[...] += part
    y_ref[0] = _layer_norm(ALPHA * x + acc_ref[...], g_ref[...], b_ref[...])


def _mlp(x, w1, w2, g, b):
    B, S, D = x.shape
    tm = PROJ_TM
    row = pl.BlockSpec((1, tm, D), lambda bb, i: (bb, i, 0))
    full = lambda a: pl.BlockSpec(a.shape, lambda bb, i: (0,) * a.ndim)
    return pl.pallas_call(
        _mlp_kernel,
        grid=(B, S // tm),
        in_specs=[row, full(w1), full(w2), full(g), full(b)],
        out_specs=row,
        out_shape=jax.ShapeDtypeStruct((B, S, D), F32),
        scratch_shapes=[pltpu.VMEM((tm, D), F32)],
        compiler_params=_params("parallel", "parallel"),
        name="mlp",
    )(x, w1, w2, g, b)


def _even_weights(w_in, cmp_pos, cmp_w1, cmp_b1, cmp_w2, pool_w, pool_scale, w_out):
    o1 = NSA_WIDTH
    o2 = o1 + 6 * NSA_KV_WIDTH
    o3 = o2 + 3 * NSA_HEADS
    D = w_in.shape[0]
    wq = (w_in[:, :o1] * NSA_DH ** -0.5).reshape(D, NSA_GROUPS, NSA_HPG, NSA_DH)
    slots = jnp.zeros((D, NSA_GROUPS, NSA_HPG, NSA_GROUPS, NSA_DH), F32)
    for g in range(NSA_GROUPS):
        slots = slots.at[:, g, :, g, :].set(wq[:, g])
    wq = slots.reshape(D, NSA_HEADS * LANES).astype(BF16)
    wkv = w_in[:, o1:o2].astype(BF16)
    wg = jnp.pad(w_in[:, o2:o3], ((0, 0), (0, LANES - 3 * NSA_HEADS))).astype(BF16)
    wu = w_in[:, o3:].astype(BF16)
    half = CMP_BLOCK // 2
    eye = jnp.eye(NSA_GROUPS, dtype=F32)
    w1 = cmp_w1.reshape(2, 2, half, NSA_DH, NSA_DH)
    w1 = jnp.einsum('whldo,gk->whlgdko', w1, eye).reshape(2, 2, half * LANES, LANES).astype(BF16)
    pos = cmp_pos.reshape(2, 2, half, 1, NSA_DH)
    pos = jnp.broadcast_to(pos, (2, 2, half, NSA_GROUPS, NSA_DH)).reshape(2, 2, 1, half * LANES)
    b1 = jnp.tile(cmp_b1, (1, NSA_GROUPS)).reshape(2, 1, LANES)
    w2 = jnp.einsum('wdo,gk->wgdko', cmp_w2, eye).reshape(2, LANES, LANES).astype(BF16)
    return dict(wq=wq, wkv=wkv, wg=wg, wu=wu, pos=pos, w1=w1, b1=b1, w2=w2,
                pw=pool_w.astype(BF16), ps=pool_scale.reshape(1, POOL_WIDTH), wo=w_out.astype(BF16))


def _selection_constants(seq_len):
    n_cmp = (seq_len - CMP_BLOCK) // CMP_STRIDE + 1
    n_sel = seq_len // SEL_BLOCK
    sub = np.arange(n_cmp)[:, None] + np.arange(CMP_BLOCK // CMP_STRIDE)[None, :]
    owner = sub // (SEL_BLOCK // CMP_STRIDE)
    sel_map = (owner[:, :, None] == np.arange(n_sel)[None, None, :]).sum(1).astype(np.float32)
    selt = np.zeros((n_sel, seq_len // CMP_STRIDE), np.float32)
    selt[:, :n_cmp] = sel_map.T
    eb = np.zeros((seq_len, LANES), np.float32)
    eb[np.arange(seq_len), np.arange(seq_len) // SEL_BLOCK] = MASK_VALUE
    return jnp.asarray(selt, BF16), jnp.asarray(eb, BF16)


def _even_mixer(x, w, ln_g, ln_b, selt, eb):
    B, S, _ = x.shape
    q, kc, vc, ks, vs, kw, vw, gates, u = _proj_even(x, w['wq'], w['wkv'], w['wg'], w['wu'], eb)
    nr = S // CMP_STRIDE
    kcc, vcc = _compress(kc.reshape(B, nr, CMP_STRIDE * LANES), vc.reshape(B, nr, CMP_STRIDE * LANES),
                         w['pos'], w['w1'], w['b1'], w['w2'])
    o = _nsa_attention(q, kcc, vcc, ks, vs, kw, vw, gates, selt)
    return _outproj_even(o, u, x, w['wo'], w['pw'], w['ps'], ln_g, ln_b)


def _odd_mixer(x, w_in, gate_w2, gate_b, norm_g, w_out, ln_g, ln_b):
    o1 = GLA_KEY_WIDTH
    o2 = o1 + GLA_KEY_WIDTH
    o3 = o2 + GLA_VAL_WIDTH
    o4 = o3 + GLA_VAL_WIDTH
    wb = w_in.astype(BF16)
    wa = jnp.pad(wb[:, o4:], ((0, 0), (0, LANES - GLA_RANK)))
    gw = jnp.pad(gate_w2.astype(BF16), ((0, LANES - GLA_RANK), (0, 0)))
    q, k, v, r, la = _proj_odd(x, wb[:, :o1], wb[:, o1:o2], wb[:, o2:o3], wb[:, o3:o4], wa, gw,
                               gate_b.reshape(1, GLA_KEY_WIDTH))
    o = _gla(q, k, la, v, r, norm_g.reshape(1, GLA_DV))
    return _outproj_odd(o, x, w_out.astype(BF16), ln_g, ln_b)


def kernel(x, a_w_in, a_cmp_pos, a_cmp_w1, a_cmp_b1, a_cmp_w2, a_pool_w, a_pool_scale, a_w_out, c_w_in, c_gate_w2, c_gate_b, c_norm_g, c_w_out, ln1_g, ln1_b, ln2_g, ln2_b, mlp_w1, mlp_w2):
    S = x.shape[1]
    selt, eb = _selection_constants(S)
    for i in range(DEPTH):
        j = i // 2
        g1, b1 = ln1_g[i].reshape(1, D_MODEL), ln1_b[i].reshape(1, D_MODEL)
        g2, b2 = ln2_g[i].reshape(1, D_MODEL), ln2_b[i].reshape(1, D_MODEL)
        if i % 2 == 0:
            w = _even_weights(a_w_in[j], a_cmp_pos[j], a_cmp_w1[j], a_cmp_b1[j], a_cmp_w2[j],
                              a_pool_w[j], a_pool_scale[j], a_w_out[j])
            x = _even_mixer(x, w, g1, b1, selt, eb)
        else:
            x = _odd_mixer(x, c_w_in[j], c_gate_w2[j], c_gate_b[j], c_norm_g[j], c_w_out[j], g1, b1)
        x = _mlp(x, mlp_w1[i].astype(BF16), mlp_w2[i].astype(BF16), g2, b2)
    return x
```

```python
import functools

import numpy as np
import jax
import jax.numpy as jnp
from jax import lax
from jax.experimental import pallas as pl
from jax.experimental.pallas import tpu as pltpu

F32 = jnp.float32
BF16 = jnp.bfloat16

D_MODEL = 1024
DEPTH = 4
NSA_HEADS = 8
NSA_GROUPS = 2
NSA_HPG = NSA_HEADS // NSA_GROUPS
NSA_DH = 64
NSA_WIDTH = NSA_HEADS * NSA_DH
NSA_KV_WIDTH = NSA_GROUPS * NSA_DH
CMP_BLOCK = 32
CMP_STRIDE = 16
SEL_BLOCK = 64
SEL_TOP_N = 16
WINDOW = 512
POOL_WINDOWS = (2, 4, 8, 16)
POOL_GROUP_DIM = 128
POOL_WIDTH = 512
GLA_HEADS = 4
GLA_DK = 128
GLA_DV = 256
GLA_KEY_WIDTH = GLA_HEADS * GLA_DK
GLA_VAL_WIDTH = GLA_HEADS * GLA_DV
GLA_RANK = 16
GLA_TAU = 16.0
GLA_CHUNK = 64
D_FF = 4 * D_MODEL
ALPHA = (2 * DEPTH) ** 0.25
LN_EPS = 1e-5
MASK_VALUE = -1e30
FORCE_SCORE = 1e4
LOG2_E = 1.4426950408889634

LANES = 128
VMEM_LIMIT_BYTES = 56 * 1024 * 1024

PROJ_TM = 512
ATT_TQ = 128
ATT_CK = 512
FF_CHUNK = 512
POOL_HALO = 16
GLA_TILE = 512
GATE_ROWS = 32
PROJ_T_ROWS = NSA_HEADS * LANES + 2 * LANES + GATE_ROWS


def _dot(a, b):
    return jnp.dot(a, b, preferred_element_type=F32)


def _dot_nt(a, b):
    return lax.dot_general(a, b, (((1,), (1,)), ((), ())), preferred_element_type=F32)


def _dot_tn(a, b):
    return lax.dot_general(a, b, (((0,), (0,)), ((), ())), preferred_element_type=F32)


def _params(*semantics):
    return pltpu.CompilerParams(dimension_semantics=semantics, vmem_limit_bytes=VMEM_LIMIT_BYTES)


def _layer_norm(z, g, b):
    mu = jnp.mean(z, axis=-1, keepdims=True)
    zc = z - mu
    var = jnp.mean(zc * zc, axis=-1, keepdims=True)
    return zc * lax.rsqrt(var + LN_EPS) * g + b


def _proj_even_kernel(x_ref, wt_ref, wr_ref, eb_ref,
                      qt_ref, vst_ref, vwt_ref, gt_ref, kc_ref, vc_ref, ks_ref, kw_ref, u_ref):
    xb = x_ref[0].astype(BF16)
    ht = _dot_nt(wt_ref[...], xb)
    nq = NSA_HEADS * LANES
    for h in range(NSA_HEADS):
        qt_ref[0, h] = ht[h * LANES:(h + 1) * LANES].astype(BF16)
    vst_ref[0] = ht[nq:nq + LANES].astype(BF16)
    vwt_ref[0] = ht[nq + LANES:nq + 2 * LANES].astype(BF16)
    gt_ref[0] = jax.nn.sigmoid(ht[nq + 2 * LANES:])
    hr = _dot(xb, wr_ref[...])
    kc_ref[0] = hr[:, 0:LANES]
    vc_ref[0] = hr[:, LANES:2 * LANES]
    ks_ref[0, :, :LANES] = hr[:, 2 * LANES:3 * LANES].astype(BF16)
    ks_ref[0, :, LANES:] = eb_ref[...]
    kw_ref[0] = hr[:, 3 * LANES:4 * LANES].astype(BF16)
    u_ref[0] = hr[:, 4 * LANES:]


def _proj_even(x, wt, wr, eb):
    B, S, D = x.shape
    tm = PROJ_TM
    row = lambda n: pl.BlockSpec((1, tm, n), lambda b, i: (b, i, 0))
    col = lambda n: pl.BlockSpec((1, n, tm), lambda b, i: (b, 0, i))
    full = lambda a: pl.BlockSpec(a.shape, lambda b, i: (0,) * a.ndim)
    out_shape = (
        jax.ShapeDtypeStruct((B, NSA_HEADS, LANES, S), BF16),
        jax.ShapeDtypeStruct((B, LANES, S), BF16),
        jax.ShapeDtypeStruct((B, LANES, S), BF16),
        jax.ShapeDtypeStruct((B, GATE_ROWS, S), F32),
        jax.ShapeDtypeStruct((B, S, LANES), F32),
        jax.ShapeDtypeStruct((B, S, LANES), F32),
        jax.ShapeDtypeStruct((B, S, 2 * LANES), BF16),
        jax.ShapeDtypeStruct((B, S, LANES), BF16),
        jax.ShapeDtypeStruct((B, S, POOL_WIDTH), F32),
    )
    out_specs = (
        pl.BlockSpec((1, NSA_HEADS, LANES, tm), lambda b, i: (b, 0, 0, i)),
        col(LANES), col(LANES), col(GATE_ROWS),
        row(LANES), row(LANES), row(2 * LANES), row(LANES), row(POOL_WIDTH),
    )
    return pl.pallas_call(
        _proj_even_kernel,
        grid=(B, S // tm),
        in_specs=[row(D), full(wt), full(wr), pl.BlockSpec((tm, LANES), lambda b, i: (i, 0))],
        out_specs=out_specs,
        out_shape=out_shape,
        compiler_params=_params("parallel", "parallel"),
        name="proj_even",
    )(x, wt, wr, eb)


def _gelu_tanh(x):
    return 0.5 * x * (1.0 + jnp.tanh(0.7978845608028654 * (x + 0.044715 * (x * x * x))))


def _compress_kernel(kc_ref, vc_ref, pos_ref, w1_ref, b1_ref, w2_ref, okc_ref, ovct_ref):
    for which, src in enumerate((kc_ref, vc_ref)):
        xr = src[0]
        nr = xr.shape[0]
        lo = _dot((xr + pos_ref[which, 0]).astype(BF16), w1_ref[which, 0])
        hi = _dot((xr + pos_ref[which, 1]).astype(BF16), w1_ref[which, 1])
        hi_next = pltpu.roll(hi, nr - 1, axis=0)
        act = _gelu_tanh(lo + hi_next + b1_ref[which])
        out = _dot(act.astype(BF16), w2_ref[which])
        rows = lax.broadcasted_iota(jnp.int32, out.shape, 0)
        out = jnp.where(rows < nr - 1, out, 0.0)
        if which == 0:
            okc_ref[0] = out.astype(BF16)
        else:
            ovct_ref[0] = out.T.astype(BF16)


def _compress(kc, vc, pos, w1, b1, w2):
    B, NR, W = kc.shape
    blk = pl.BlockSpec((1, NR, W), lambda b: (b, 0, 0))
    full = lambda a: pl.BlockSpec(a.shape, lambda b: (0,) * a.ndim)
    return pl.pallas_call(
        _compress_kernel,
        grid=(B,),
        in_specs=[blk, blk, full(pos), full(w1), full(b1), full(w2)],
        out_specs=(pl.BlockSpec((1, NR, LANES), lambda b: (b, 0, 0)),
                   pl.BlockSpec((1, LANES, NR), lambda b: (b, 0, 0))),
        out_shape=(jax.ShapeDtypeStruct((B, NR, LANES), BF16),
                   jax.ShapeDtypeStruct((B, LANES, NR), BF16)),
        compiler_params=_params("parallel"),
        name="compress",
    )(kc, vc, pos, w1, b1, w2)


def _nsa_kernel(qt_ref, kc_ref, vct_ref, ks_ref, vst_ref, kw_ref, vwt_ref, gt_ref, selt_ref, o_ref,
                qsel_ref, oc_ref, acc_ref, *, seq_len):
    tq = ATT_TQ
    cols = NSA_HPG * tq
    n_sel = seq_len // SEL_BLOCK
    t0 = pl.program_id(1) * tq
    t_q = t0 + lax.broadcasted_iota(jnp.int32, (1, tq), 1)
    t_cols = jnp.concatenate([t_q] * NSA_HPG, axis=1)

    for g in range(NSA_GROUPS):
        qt = jnp.concatenate([qt_ref[0, NSA_HPG * g + h] for h in range(NSA_HPG)], axis=1)

        s_c = _dot(kc_ref[0], qt)
        c_end = lax.broadcasted_iota(jnp.int32, s_c.shape, 0) * CMP_STRIDE + (CMP_BLOCK - 1)
        valid_c = c_end <= t_cols
        s_c = jnp.where(valid_c, s_c, MASK_VALUE)
        e_c = jnp.where(valid_c, jnp.exp2(s_c - jnp.max(s_c, axis=0, keepdims=True)), 0.0)
        p_c = e_c * (1.0 / jnp.maximum(jnp.sum(e_c, axis=0, keepdims=True), 1e-30))
        oc_ref[g] = _dot(vct_ref[0], p_c.astype(BF16))

        p_sum = p_c[:, 0:tq]
        for h in range(1, NSA_HPG):
            p_sum = p_sum + p_c[:, h * tq:(h + 1) * tq]
        p_hi = p_sum.astype(BF16)
        p_lo = (p_sum - p_hi.astype(F32)).astype(BF16)
        imp = _dot(selt_ref[...], p_hi) + _dot(selt_ref[...], p_lo)
        blk = lax.broadcasted_iota(jnp.int32, imp.shape, 0)
        cur = t_q >> 6
        forced = (blk == 0) | (blk == cur) | (blk == cur - 1)
        score = jnp.where(forced, FORCE_SCORE, jnp.where(blk > cur, -FORCE_SCORE, imp))
        beaten = jnp.zeros(imp.shape, F32)
        for i in range(n_sel):
            s_i = score[i:i + 1, :]
            wins = (s_i > score) | ((s_i == score) & (blk > i))
            beaten = beaten + jnp.where(wins, 1.0, 0.0)
        not_sel = jnp.where(beaten < float(SEL_TOP_N), 0.0, 1.0)
        not_sel = jnp.concatenate([not_sel, jnp.zeros((LANES - n_sel, tq), F32)], axis=0).astype(BF16)
        qsel_ref[g, 0:LANES, :] = qt
        qsel_ref[g, LANES:, :] = jnp.concatenate([not_sel] * NSA_HPG, axis=1)

    acc_ref[...] = jnp.zeros(acc_ref.shape, F32)

    def chunk(c, carry, on_diagonal):
        k0 = pl.multiple_of(c * ATT_CK, ATT_CK)
        k_blk = ks_ref[0, pl.ds(k0, ATT_CK), :]
        v_blk = vst_ref[0, :, pl.ds(k0, ATT_CK)]
        if on_diagonal:
            causal = (k0 + lax.broadcasted_iota(jnp.int32, (ATT_CK, cols), 0)) <= t_cols
        out = []
        for g in range(NSA_GROUPS):
            m_prev, l_prev = carry[2 * g], carry[2 * g + 1]
            s = _dot(k_blk, qsel_ref[g])
            if on_diagonal:
                s = jnp.where(causal, s, MASK_VALUE)
            m_new = jnp.maximum(m_prev, jnp.max(s, axis=0, keepdims=True))
            scale = jnp.exp2(m_prev - m_new)
            p = jnp.exp2(s - m_new)
            out += [m_new, scale * l_prev + jnp.sum(p, axis=0, keepdims=True)]
            acc_ref[g] = scale * acc_ref[g] + _dot(v_blk, p.astype(BF16))
        return tuple(out)

    last = (t0 + tq - 1) // ATT_CK
    m_init = jnp.full((1, cols), MASK_VALUE, F32)
    l_init = jnp.zeros((1, cols), F32)
    stats = lax.fori_loop(0, last, functools.partial(chunk, on_diagonal=False),
                          (m_init, l_init) * NSA_GROUPS)
    stats = chunk(last, stats, on_diagonal=True)

    gates = gt_ref[0]
    w0 = pl.multiple_of(jnp.maximum(t0 - WINDOW, 0), tq)
    wlen = WINDOW + tq
    for g in range(NSA_GROUPS):
        o_s = acc_ref[g] * (1.0 / jnp.maximum(stats[2 * g + 1], 1e-30))

        s_w = _dot(kw_ref[0, pl.ds(w0, wlen), :], qsel_ref[g, 0:LANES, :])
        wpos = w0 + lax.broadcasted_iota(jnp.int32, s_w.shape, 0)
        s_w = jnp.where((wpos <= t_cols) & (wpos > t_cols - WINDOW), s_w, MASK_VALUE)
        e_w = jnp.exp2(s_w - jnp.max(s_w, axis=0, keepdims=True))
        o_w = _dot(vwt_ref[0, :, pl.ds(w0, wlen)], e_w.astype(BF16))
        o_w = o_w * (1.0 / jnp.maximum(jnp.sum(e_w, axis=0, keepdims=True), 1e-30))

        o_c = oc_ref[g]
        heads = []
        for h in range(NSA_HPG):
            c = slice(h * tq, (h + 1) * tq)
            gi = (NSA_HPG * g + h) * 3
            mix = (gates[gi:gi + 1] * o_c[:, c] + gates[gi + 1:gi + 2] * o_s[:, c]
                   + gates[gi + 2:gi + 3] * o_w[:, c])
            heads.append(mix[g * NSA_DH:(g + 1) * NSA_DH])
        for j in range(NSA_HPG // 2):
            pair = jnp.concatenate([heads[2 * j], heads[2 * j + 1]], axis=0)
            c0 = (NSA_HPG * g + 2 * j) * NSA_DH
            o_ref[0, :, c0:c0 + LANES] = pair.T.astype(BF16)


def _nsa_attention(qt, kc, vct, ks, vst, kw, vwt, gt, selt):
    B, _, _, S = qt.shape
    tq = ATT_TQ
    cols = NSA_HPG * tq
    seq_rows = lambda n: pl.BlockSpec((1, S, n), lambda b, i: (b, 0, 0))
    seq_cols = lambda n: pl.BlockSpec((1, n, S), lambda b, i: (b, 0, 0))
    whole = lambda a: pl.BlockSpec((1,) + a.shape[1:], lambda b, i: (b, 0, 0))
    return pl.pallas_call(
        functools.partial(_nsa_kernel, seq_len=S),
        grid=(B, S // tq),
        in_specs=[pl.BlockSpec((1, NSA_HEADS, LANES, tq), lambda b, i: (b, 0, 0, i)),
                  whole(kc), whole(vct), seq_rows(2 * LANES), seq_cols(LANES), seq_rows(LANES),
                  seq_cols(LANES), pl.BlockSpec((1, GATE_ROWS, tq), lambda b, i: (b, 0, i)),
                  pl.BlockSpec(selt.shape, lambda b, i: (0, 0))],
        out_specs=pl.BlockSpec((1, tq, NSA_WIDTH), lambda b, i: (b, i, 0)),
        out_shape=jax.ShapeDtypeStruct((B, S, NSA_WIDTH), BF16),
        scratch_shapes=[pltpu.VMEM((NSA_GROUPS, 2 * LANES, cols), BF16),
                        pltpu.VMEM((NSA_GROUPS, LANES, cols), F32),
                        pltpu.VMEM((NSA_GROUPS, LANES, cols), F32)],
        compiler_params=_params("parallel", "parallel"),
        name="nsa_attention",
    )(qt, kc, vct, ks, vst, kw, vwt, gt, selt)


def _outproj_even_kernel(o_ref, u_ref, x_ref, wo_ref, pw_ref, ps_ref, g_ref, b_ref, y_ref, ext_ref):
    tm = u_ref.shape[1]
    i = pl.program_id(1)

    @pl.when(i == 0)
    def _():
        ext_ref[0:POOL_HALO, :] = jnp.zeros((POOL_HALO, POOL_WIDTH), F32)

    u = u_ref[0]
    ext_ref[POOL_HALO:, :] = u
    t = (i * tm + lax.broadcasted_iota(jnp.int32, (tm, 1), 0) + 1).astype(F32)
    mix = _dot(o_ref[0], wo_ref[0:NSA_WIDTH, :])
    for gi, w in enumerate(POOL_WINDOWS):
        cols = slice(gi * POOL_GROUP_DIM, (gi + 1) * POOL_GROUP_DIM)
        u_g = u[:, cols]
        acc = u_g
        for d in range(1, w):
            acc = acc + ext_ref[POOL_HALO - d:POOL_HALO - d + tm, cols]
        r = acc / jnp.minimum(t, float(w)) - u_g
        y_g = _dot(r.astype(BF16), pw_ref[gi]) * ps_ref[:, cols]
        mix = mix + _dot(y_g.astype(BF16), wo_ref[NSA_WIDTH + gi * POOL_GROUP_DIM:
                                                  NSA_WIDTH + (gi + 1) * POOL_GROUP_DIM, :])
    ext_ref[0:POOL_HALO, :] = u[tm - POOL_HALO:, :]
    y_ref[0] = _layer_norm(ALPHA * x_ref[0] + mix, g_ref[...], b_ref[...])


def _outproj_even(o, u, x, wo, pw, ps, g, b):
    B, S, D = x.shape
    tm = PROJ_TM
    row = lambda n: pl.BlockSpec((1, tm, n), lambda bb, i: (bb, i, 0))
    full = lambda a: pl.BlockSpec(a.shape, lambda bb, i: (0,) * a.ndim)
    return pl.pallas_call(
        _outproj_even_kernel,
        grid=(B, S // tm),
        in_specs=[row(NSA_WIDTH), row(POOL_WIDTH), row(D), full(wo), full(pw), full(ps), full(g), full(b)],
        out_specs=row(D),
        out_shape=jax.ShapeDtypeStruct((B, S, D), F32),
        scratch_shapes=[pltpu.VMEM((POOL_HALO + tm, POOL_WIDTH), F32)],
        compiler_params=_params("parallel", "arbitrary"),
        name="outproj_even",
    )(o, u, x, wo, pw, ps, g, b)


def _proj_odd_kernel(x_ref, wq_ref, wk_ref, wv_ref, wr_ref, wa_ref, gw_ref, gb_ref,
                     q_ref, k_ref, v_ref, r_ref, la_ref):
    xb = x_ref[0].astype(BF16)
    n = FF_CHUNK
    for c in range(GLA_KEY_WIDTH // n):
        q_ref[0, :, c * n:(c + 1) * n] = _dot(xb, wq_ref[:, c * n:(c + 1) * n])
        k_ref[0, :, c * n:(c + 1) * n] = _dot(xb, wk_ref[:, c * n:(c + 1) * n])
    for c in range(GLA_VAL_WIDTH // n):
        v_ref[0, :, c * n:(c + 1) * n] = _dot(xb, wv_ref[:, c * n:(c + 1) * n]).astype(BF16)
        r_ref[0, :, c * n:(c + 1) * n] = _dot(xb, wr_ref[:, c * n:(c + 1) * n])
    a = _dot(xb, wa_ref[...])
    z = _dot(a.astype(BF16), gw_ref[...]) + gb_ref[...]
    log_sig = jnp.minimum(z, 0.0) - jnp.log(1.0 + jnp.exp(-jnp.abs(z)))
    la_ref[0] = log_sig / GLA_TAU


def _proj_odd(x, wq, wk, wv, wr, wa, gw, gb):
    B, S, D = x.shape
    tm = PROJ_TM
    row = lambda n: pl.BlockSpec((1, tm, n), lambda b, i: (b, i, 0))
    full = lambda a: pl.BlockSpec(a.shape, lambda b, i: (0,) * a.ndim)
    return pl.pallas_call(
        _proj_odd_kernel,
        grid=(B, S // tm),
        in_specs=[row(D), full(wq), full(wk), full(wv), full(wr), full(wa), full(gw), full(gb)],
        out_specs=(row(GLA_KEY_WIDTH), row(GLA_KEY_WIDTH), row(GLA_VAL_WIDTH), row(GLA_VAL_WIDTH),
                   row(GLA_KEY_WIDTH)),
        out_shape=(jax.ShapeDtypeStruct((B, S, GLA_KEY_WIDTH), F32),
                   jax.ShapeDtypeStruct((B, S, GLA_KEY_WIDTH), F32),
                   jax.ShapeDtypeStruct((B, S, GLA_VAL_WIDTH), BF16),
                   jax.ShapeDtypeStruct((B, S, GLA_VAL_WIDTH), F32),
                   jax.ShapeDtypeStruct((B, S, GLA_KEY_WIDTH), F32)),
        compiler_params=_params("parallel", "parallel"),
        name="proj_odd",
    )(x, wq, wk, wv, wr, wa, gw, gb)


def _gla_kernel(q_ref, k_ref, la_ref, v_ref, r_ref, ng_ref, o_ref, state_ref):
    C = GLA_CHUNK

    @pl.when(pl.program_id(1) == 0)
    def _():
        state_ref[...] = jnp.zeros(state_ref.shape, F32)

    row = lax.broadcasted_iota(jnp.int32, (C, GLA_DK), 0)
    causal = (lax.broadcasted_iota(jnp.int32, (C, C), 0) >= lax.broadcasted_iota(jnp.int32, (C, C), 1))
    for n in range(q_ref.shape[1] // C):
        rows = slice(n * C, (n + 1) * C)
        for h in range(GLA_HEADS):
            kcols = slice(h * GLA_DK, (h + 1) * GLA_DK)
            vcols = slice(h * GLA_DV, (h + 1) * GLA_DV)
            b = la_ref[0, rows, kcols]
            shift = 1
            while shift < C:
                b = b + jnp.where(row >= shift, pltpu.roll(b, shift, axis=0), 0.0)
                shift *= 2
            b_last = b[C - 1:C, :]
            q_t = (q_ref[0, rows, kcols] * (GLA_DK ** -0.5)) * jnp.exp(b)
            k = k_ref[0, rows, kcols]
            k_t = k * jnp.exp(-b)
            k_u = k * jnp.exp(b_last - b)
            v = v_ref[0, rows, vcols]
            q_tb = q_t.astype(BF16)
            att = jnp.where(causal, _dot_nt(q_tb, k_t.astype(BF16)), 0.0)
            state = state_ref[h]
            o = _dot(att.astype(BF16), v) + _dot_nt(q_tb, state.astype(BF16))
            upd = _dot_tn(v, k_u.astype(BF16))
            state_ref[h] = jnp.exp(b_last) * state + upd
            o = o * lax.rsqrt(jnp.mean(o * o, axis=-1, keepdims=True) + LN_EPS) * ng_ref[...]
            o_ref[0, rows, vcols] = (o * jax.nn.silu(r_ref[0, rows, vcols])).astype(BF16)


def _gla(q, k, la, v, r, ng):
    B, S, _ = q.shape
    row = lambda n: pl.BlockSpec((1, GLA_TILE, n), lambda b, i: (b, i, 0))
    return pl.pallas_call(
        _gla_kernel,
        grid=(B, S // GLA_TILE),
        in_specs=[row(GLA_KEY_WIDTH), row(GLA_KEY_WIDTH), row(GLA_KEY_WIDTH), row(GLA_VAL_WIDTH),
                  row(GLA_VAL_WIDTH), pl.BlockSpec(ng.shape, lambda b, i: (0, 0))],
        out_specs=row(GLA_VAL_WIDTH),
        out_shape=jax.ShapeDtypeStruct((B, S, GLA_VAL_WIDTH), BF16),
        scratch_shapes=[pltpu.VMEM((GLA_HEADS, GLA_DV, GLA_DK), F32)],
        compiler_params=_params("parallel", "arbitrary"),
        name="gla",
    )(q, k, la, v, r, ng)


def _outproj_odd_kernel(o_ref, x_ref, wo_ref, g_ref, b_ref, y_ref):
    mix = _dot(o_ref[0], wo_ref[...])
    y_ref[0] = _layer_norm(ALPHA * x_ref[0] + mix, g_ref[...], b_ref[...])


def _outproj_odd(o, x, wo, g, b):
    B, S, D = x.shape
    tm = PROJ_TM
    row = lambda n: pl.BlockSpec((1, tm, n), lambda bb, i: (bb, i, 0))
    full = lambda a: pl.BlockSpec(a.shape, lambda bb, i: (0,) * a.ndim)
    return pl.pallas_call(
        _outproj_odd_kernel,
        grid=(B, S // tm),
        in_specs=[row(GLA_VAL_WIDTH), row(D), full(wo), full(g), full(b)],
        out_specs=row(D),
        out_shape=jax.ShapeDtypeStruct((B, S, D), F32),
        compiler_params=_params("parallel", "parallel"),
        name="outproj_odd",
    )(o, x, wo, g, b)


def _mlp_kernel(x_ref, w1_ref, w2_ref, g_ref, b_ref, y_ref, acc_ref):
    x = x_ref[0]
    xb = x.astype(BF16)
    for c in range(D_FF // FF_CHUNK):
        cols = slice(c * FF_CHUNK, (c + 1) * FF_CHUNK)
        h = jnp.maximum(_dot(xb, w1_ref[:, cols]), 0.0)
        part = _dot((h * h).astype(BF16), w2_ref[cols, :])
        if c == 0:
            acc_ref[...] = part
        else:
            acc_ref[...] += part
    y_ref[0] = _layer_norm(ALPHA * x + acc_ref[...], g_ref[...], b_ref[...])


def _mlp(x, w1, w2, g, b):
    B, S, D = x.shape
    tm = PROJ_TM
    row = pl.BlockSpec((1, tm, D), lambda bb, i: (bb, i, 0))
    full = lambda a: pl.BlockSpec(a.shape, lambda bb, i: (0,) * a.ndim)
    return pl.pallas_call(
        _mlp_kernel,
        grid=(B, S // tm),
        in_specs=[row, full(w1), full(w2), full(g), full(b)],
        out_specs=row,
        out_shape=jax.ShapeDtypeStruct((B, S, D), F32),
        scratch_shapes=[pltpu.VMEM((tm, D), F32)],
        compiler_params=_params("parallel", "parallel"),
        name="mlp",
    )(x, w1, w2, g, b)


def _even_weights(w_in, cmp_pos, cmp_w1, cmp_b1, cmp_w2, pool_w, pool_scale, w_out):
    o1 = NSA_WIDTH
    o2 = o1 + 6 * NSA_KV_WIDTH
    o3 = o2 + 3 * NSA_HEADS
    D = w_in.shape[0]
    wq = (w_in[:, :o1] * (NSA_DH ** -0.5 * LOG2_E)).reshape(D, NSA_GROUPS, NSA_HPG, NSA_DH)
    slots = jnp.zeros((D, NSA_GROUPS, NSA_HPG, NSA_GROUPS, NSA_DH), F32)
    for g in range(NSA_GROUPS):
        slots = slots.at[:, g, :, g, :].set(wq[:, g])
    wq = slots.reshape(D, NSA_HEADS * LANES)
    kv = [w_in[:, o1 + n * NSA_KV_WIDTH:o1 + (n + 1) * NSA_KV_WIDTH] for n in range(6)]
    k_cmp, v_cmp, k_slc, v_slc, k_win, v_win = kv
    wg = jnp.pad(w_in[:, o2:o3], ((0, 0), (0, GATE_ROWS - 3 * NSA_HEADS)))
    wt = jnp.concatenate([wq, v_slc, v_win, wg], axis=1).T.astype(BF16)
    wr = jnp.concatenate([k_cmp, v_cmp, k_slc, k_win, w_in[:, o3:]], axis=1).astype(BF16)
    half = CMP_BLOCK // 2
    eye = jnp.eye(NSA_GROUPS, dtype=F32)
    w1 = cmp_w1.reshape(2, 2, half, NSA_DH, NSA_DH)
    w1 = jnp.einsum('whldo,gk->whlgdko', w1, eye).reshape(2, 2, half * LANES, LANES).astype(BF16)
    pos = cmp_pos.reshape(2, 2, half, 1, NSA_DH)
    pos = jnp.broadcast_to(pos, (2, 2, half, NSA_GROUPS, NSA_DH)).reshape(2, 2, 1, half * LANES)
    b1 = jnp.tile(cmp_b1, (1, NSA_GROUPS)).reshape(2, 1, LANES)
    w2 = jnp.einsum('wdo,gk->wgdko', cmp_w2, eye).reshape(2, LANES, LANES).astype(BF16)
    return dict(wt=wt, wr=wr, pos=pos, w1=w1, b1=b1, w2=w2,
                pw=pool_w.astype(BF16), ps=pool_scale.reshape(1, POOL_WIDTH), wo=w_out.astype(BF16))


def _selection_constants(seq_len):
    n_cmp = (seq_len - CMP_BLOCK) // CMP_STRIDE + 1
    n_sel = seq_len // SEL_BLOCK
    sub = np.arange(n_cmp)[:, None] + np.arange(CMP_BLOCK // CMP_STRIDE)[None, :]
    owner = sub // (SEL_BLOCK // CMP_STRIDE)
    sel_map = (owner[:, :, None] == np.arange(n_sel)[None, None, :]).sum(1).astype(np.float32)
    selt = np.zeros((n_sel, seq_len // CMP_STRIDE), np.float32)
    selt[:, :n_cmp] = sel_map.T
    eb = np.zeros((seq_len, LANES), np.float32)
    eb[np.arange(seq_len), np.arange(seq_len) // SEL_BLOCK] = MASK_VALUE
    return jnp.asarray(selt, BF16), jnp.asarray(eb, BF16)


def _even_mixer(x, w, ln_g, ln_b, selt, eb):
    B, S, _ = x.shape
    qt, vst, vwt, gt, kc, vc, ks, kw, u = _proj_even(x, w['wt'], w['wr'], eb)
    nr = S // CMP_STRIDE
    kcc, vcct = _compress(kc.reshape(B, nr, CMP_STRIDE * LANES), vc.reshape(B, nr, CMP_STRIDE * LANES),
                          w['pos'], w['w1'], w['b1'], w['w2'])
    o = _nsa_attention(qt, kcc, vcct, ks, vst, kw, vwt, gt, selt)
    return _outproj_even(o, u, x, w['wo'], w['pw'], w['ps'], ln_g, ln_b)


def _odd_mixer(x, w_in, gate_w2, gate_b, norm_g, w_out, ln_g, ln_b):
    o1 = GLA_KEY_WIDTH
    o2 = o1 + GLA_KEY_WIDTH
    o3 = o2 + GLA_VAL_WIDTH
    o4 = o3 + GLA_VAL_WIDTH
    wb = w_in.astype(BF16)
    wa = jnp.pad(wb[:, o4:], ((0, 0), (0, LANES - GLA_RANK)))
    gw = jnp.pad(gate_w2.astype(BF16), ((0, LANES - GLA_RANK), (0, 0)))
    q, k, v, r, la = _proj_odd(x, wb[:, :o1], wb[:, o1:o2], wb[:, o2:o3], wb[:, o3:o4], wa, gw,
                               gate_b.reshape(1, GLA_KEY_WIDTH))
    o = _gla(q, k, la, v, r, norm_g.reshape(1, GLA_DV))
    return _outproj_odd(o, x, w_out.astype(BF16), ln_g, ln_b)


def kernel(x, a_w_in, a_cmp_pos, a_cmp_w1, a_cmp_b1, a_cmp_w2, a_pool_w, a_pool_scale, a_w_out, c_w_in, c_gate_w2, c_gate_b, c_norm_g, c_w_out, ln1_g, ln1_b, ln2_g, ln2_b, mlp_w1, mlp_w2):
    S = x.shape[1]
    selt, eb = _selection_constants(S)
    for i in range(DEPTH):
        j = i // 2
        g1, b1 = ln1_g[i].reshape(1, D_MODEL), ln1_b[i].reshape(1, D_MODEL)
        g2, b2 = ln2_g[i].reshape(1, D_MODEL), ln2_b[i].reshape(1, D_MODEL)
        if i % 2 == 0:
            w = _even_weights(a_w_in[j], a_cmp_pos[j], a_cmp_w1[j], a_cmp_b1[j], a_cmp_w2[j],
                              a_pool_w[j], a_pool_scale[j], a_w_out[j])
            x = _even_mixer(x, w, g1, b1, selt, eb)
        else:
            x = _odd_mixer(x, c_w_in[j], c_gate_w2[j], c_gate_b[j], c_norm_g[j], c_w_out[j], g1, b1)
        x = _mlp(x, mlp_w1[i].astype(BF16), mlp_w2[i].astype(BF16), g2, b2)
    return x
```

```python
import functools

import numpy as np
import jax
import jax.numpy as jnp
from jax import lax
from jax.experimental import pallas as pl
from jax.experimental.pallas import tpu as pltpu

F32 = jnp.float32
BF16 = jnp.bfloat16

D_MODEL = 1024
DEPTH = 4
NSA_HEADS = 8
NSA_GROUPS = 2
NSA_HPG = NSA_HEADS // NSA_GROUPS
NSA_DH = 64
NSA_WIDTH = NSA_HEADS * NSA_DH
NSA_KV_WIDTH = NSA_GROUPS * NSA_DH
CMP_BLOCK = 32
CMP_STRIDE = 16
SEL_BLOCK = 64
SEL_TOP_N = 16
WINDOW = 512
POOL_WINDOWS = (2, 4, 8, 16)
POOL_GROUP_DIM = 128
POOL_WIDTH = 512
GLA_HEADS = 4
GLA_DK = 128
GLA_DV = 256
GLA_KEY_WIDTH = GLA_HEADS * GLA_DK
GLA_VAL_WIDTH = GLA_HEADS * GLA_DV
GLA_RANK = 16
GLA_TAU = 16.0
GLA_CHUNK = 64
D_FF = 4 * D_MODEL
ALPHA = (2 * DEPTH) ** 0.25
LN_EPS = 1e-5
MASK_VALUE = -1e30
FORCE_SCORE = 1e4
LOG2_E = 1.4426950408889634

LANES = 128
VMEM_LIMIT_BYTES = 56 * 1024 * 1024

PROJ_TM = 512
ATT_TQ = 256
ATT_CK = 512
FF_CHUNK = 512
POOL_HALO = 16
GLA_TILE = 512
GATE_ROWS = 32
PROJ_T_ROWS = NSA_HEADS * LANES + 2 * LANES + GATE_ROWS


def _dot(a, b):
    return jnp.dot(a, b, preferred_element_type=F32)


def _dot_nt(a, b):
    return lax.dot_general(a, b, (((1,), (1,)), ((), ())), preferred_element_type=F32)


def _dot_tn(a, b):
    return lax.dot_general(a, b, (((0,), (0,)), ((), ())), preferred_element_type=F32)


def _params(*semantics):
    return pltpu.CompilerParams(dimension_semantics=semantics, vmem_limit_bytes=VMEM_LIMIT_BYTES)


def _layer_norm(z, g, b):
    mu = jnp.mean(z, axis=-1, keepdims=True)
    zc = z - mu
    var = jnp.mean(zc * zc, axis=-1, keepdims=True)
    return zc * lax.rsqrt(var + LN_EPS) * g + b


def _proj_even_kernel(x_ref, wt_ref, wr_ref, eb_ref,
                      qt_ref, vst_ref, vwt_ref, gt_ref, kc_ref, vc_ref, ks_ref, kw_ref, u_ref):
    xb = x_ref[0].astype(BF16)
    ht = _dot_nt(wt_ref[...], xb)
    nq = NSA_HEADS * LANES
    for h in range(NSA_HEADS):
        qt_ref[0, h] = ht[h * LANES:(h + 1) * LANES].astype(BF16)
    rows = lax.broadcasted_iota(jnp.int32, (LANES, ht.shape[1]), 0)
    for g in range(NSA_GROUPS):
        ones_row = rows == (1 - g) * NSA_DH
        vst_ref[0, g] = jnp.where(ones_row, 1.0, ht[nq:nq + LANES]).astype(BF16)
        vwt_ref[0, g] = jnp.where(ones_row, 1.0, ht[nq + LANES:nq + 2 * LANES]).astype(BF16)
    gt_ref[0] = jax.nn.sigmoid(ht[nq + 2 * LANES:])
    hr = _dot(xb, wr_ref[...])
    kc_ref[0] = hr[:, 0:LANES]
    vc_ref[0] = hr[:, LANES:2 * LANES]
    ks_ref[0, :, :LANES] = hr[:, 2 * LANES:3 * LANES].astype(BF16)
    ks_ref[0, :, LANES:] = eb_ref[...]
    kw_ref[0] = hr[:, 3 * LANES:4 * LANES].astype(BF16)
    u_ref[0] = hr[:, 4 * LANES:]


def _proj_even(x, wt, wr, eb):
    B, S, D = x.shape
    tm = PROJ_TM
    row = lambda n: pl.BlockSpec((1, tm, n), lambda b, i: (b, i, 0))
    col = lambda n: pl.BlockSpec((1, n, tm), lambda b, i: (b, 0, i))
    full = lambda a: pl.BlockSpec(a.shape, lambda b, i: (0,) * a.ndim)
    out_shape = (
        jax.ShapeDtypeStruct((B, NSA_HEADS, LANES, S), BF16),
        jax.ShapeDtypeStruct((B, NSA_GROUPS, LANES, S), BF16),
        jax.ShapeDtypeStruct((B, NSA_GROUPS, LANES, S), BF16),
        jax.ShapeDtypeStruct((B, GATE_ROWS, S), F32),
        jax.ShapeDtypeStruct((B, S, LANES), F32),
        jax.ShapeDtypeStruct((B, S, LANES), F32),
        jax.ShapeDtypeStruct((B, S, 2 * LANES), BF16),
        jax.ShapeDtypeStruct((B, S, LANES), BF16),
        jax.ShapeDtypeStruct((B, S, POOL_WIDTH), F32),
    )
    out_specs = (
        pl.BlockSpec((1, NSA_HEADS, LANES, tm), lambda b, i: (b, 0, 0, i)),
        pl.BlockSpec((1, NSA_GROUPS, LANES, tm), lambda b, i: (b, 0, 0, i)),
        pl.BlockSpec((1, NSA_GROUPS, LANES, tm), lambda b, i: (b, 0, 0, i)), col(GATE_ROWS),
        row(LANES), row(LANES), row(2 * LANES), row(LANES), row(POOL_WIDTH),
    )
    return pl.pallas_call(
        _proj_even_kernel,
        grid=(B, S // tm),
        in_specs=[row(D), full(wt), full(wr), pl.BlockSpec((tm, LANES), lambda b, i: (i, 0))],
        out_specs=out_specs,
        out_shape=out_shape,
        compiler_params=_params("parallel", "parallel"),
        name="proj_even",
    )(x, wt, wr, eb)


def _gelu_tanh(x):
    return 0.5 * x * (1.0 + jnp.tanh(0.7978845608028654 * (x + 0.044715 * (x * x * x))))


def _compress_kernel(kc_ref, vc_ref, pos_ref, w1_ref, b1_ref, w2_ref, okc_ref, ovct_ref):
    for which, src in enumerate((kc_ref, vc_ref)):
        xr = src[0]
        nr = xr.shape[0]
        lo = _dot((xr + pos_ref[which, 0]).astype(BF16), w1_ref[which, 0])
        hi = _dot((xr + pos_ref[which, 1]).astype(BF16), w1_ref[which, 1])
        hi_next = pltpu.roll(hi, nr - 1, axis=0)
        act = _gelu_tanh(lo + hi_next + b1_ref[which])
        out = _dot(act.astype(BF16), w2_ref[which])
        rows = lax.broadcasted_iota(jnp.int32, out.shape, 0)
        out = jnp.where(rows < nr - 1, out, 0.0)
        if which == 0:
            okc_ref[0] = out.astype(BF16)
        else:
            ovct_ref[0] = out.T.astype(BF16)


def _compress(kc, vc, pos, w1, b1, w2):
    B, NR, W = kc.shape
    blk = pl.BlockSpec((1, NR, W), lambda b: (b, 0, 0))
    full = lambda a: pl.BlockSpec(a.shape, lambda b: (0,) * a.ndim)
    return pl.pallas_call(
        _compress_kernel,
        grid=(B,),
        in_specs=[blk, blk, full(pos), full(w1), full(b1), full(w2)],
        out_specs=(pl.BlockSpec((1, NR, LANES), lambda b: (b, 0, 0)),
                   pl.BlockSpec((1, LANES, NR), lambda b: (b, 0, 0))),
        out_shape=(jax.ShapeDtypeStruct((B, NR, LANES), BF16),
                   jax.ShapeDtypeStruct((B, LANES, NR), BF16)),
        compiler_params=_params("parallel"),
        name="compress",
    )(kc, vc, pos, w1, b1, w2)


def _nsa_kernel(qt_ref, kc_ref, vct_ref, ks_ref, vst_ref, kw_ref, vwt_ref, gt_ref, selt_ref, o_ref,
                qsel_ref, oc_ref, s_ref, acc_ref, *, seq_len):
    tq = ATT_TQ
    cols = NSA_HPG * tq
    n_sel = seq_len // SEL_BLOCK
    t0 = pl.program_id(1) * tq
    t_q = t0 + lax.broadcasted_iota(jnp.int32, (1, tq), 1)
    t_cols = jnp.concatenate([t_q] * NSA_HPG, axis=1)

    for g in range(NSA_GROUPS):
        qt = jnp.concatenate([qt_ref[0, NSA_HPG * g + h] for h in range(NSA_HPG)], axis=1)

        s_c = _dot(kc_ref[0], qt)
        c_end = lax.broadcasted_iota(jnp.int32, s_c.shape, 0) * CMP_STRIDE + (CMP_BLOCK - 1)
        valid_c = c_end <= t_cols
        s_c = jnp.where(valid_c, s_c, MASK_VALUE)
        e_c = jnp.where(valid_c, jnp.exp2(s_c - jnp.max(s_c, axis=0, keepdims=True)), 0.0)
        p_c = e_c * (1.0 / jnp.maximum(jnp.sum(e_c, axis=0, keepdims=True), 1e-30))
        oc_ref[g] = _dot(vct_ref[0], p_c.astype(BF16))

        p_sum = p_c[:, 0:tq]
        for h in range(1, NSA_HPG):
            p_sum = p_sum + p_c[:, h * tq:(h + 1) * tq]
        p_hi = p_sum.astype(BF16)
        p_lo = (p_sum - p_hi.astype(F32)).astype(BF16)
        imp = _dot(selt_ref[...], p_hi) + _dot(selt_ref[...], p_lo)
        blk = lax.broadcasted_iota(jnp.int32, imp.shape, 0)
        cur = t_q >> 6
        forced = (blk == 0) | (blk == cur) | (blk == cur - 1)
        score = jnp.where(forced, FORCE_SCORE, jnp.where(blk > cur, -FORCE_SCORE, imp))
        beaten = jnp.zeros(imp.shape, F32)
        for i in range(n_sel):
            s_i = score[i:i + 1, :]
            wins = (s_i > score) | ((s_i == score) & (blk > i))
            beaten = beaten + jnp.where(wins, 1.0, 0.0)
        not_sel = jnp.where(beaten < float(SEL_TOP_N), 0.0, 1.0)
        not_sel = jnp.concatenate([not_sel, jnp.zeros((LANES - n_sel, tq), F32)], axis=0).astype(BF16)
        qsel_ref[g, 0:LANES, :] = qt
        qsel_ref[g, LANES:, :] = jnp.concatenate([not_sel] * NSA_HPG, axis=1)

    def score(g, c, m, on_diagonal):
        k0 = pl.multiple_of(c * ATT_CK, ATT_CK)
        s = _dot(ks_ref[0, pl.ds(k0, ATT_CK), :], qsel_ref[g])
        if on_diagonal:
            kpos = k0 + lax.broadcasted_iota(jnp.int32, s.shape, 0)
            s = jnp.where(kpos <= t_cols, s, MASK_VALUE)
        s_ref[g, pl.ds(k0, ATT_CK), :] = s
        return jnp.maximum(m, jnp.max(s, axis=0, keepdims=True))

    def attend(g, c, m):
        k0 = pl.multiple_of(c * ATT_CK, ATT_CK)
        p = jnp.exp2(s_ref[g, pl.ds(k0, ATT_CK), :] - m).astype(BF16)
        acc_ref[g] += _dot(vst_ref[0, g, :, pl.ds(k0, ATT_CK)], p)

    last = (t0 + tq - 1) // ATT_CK
    acc_ref[...] = jnp.zeros(acc_ref.shape, F32)
    m_init = jnp.full((1, cols), MASK_VALUE, F32)
    m0 = lax.fori_loop(0, last, lambda c, m: score(0, c, m, False), m_init)
    m0 = score(0, last, m0, True)

    def score1_attend0(c, m, on_diagonal):
        m = score(1, c, m, on_diagonal)
        attend(0, c, m0)
        return m

    m1 = lax.fori_loop(0, last, lambda c, m: score1_attend0(c, m, False), m_init)
    m1 = score1_attend0(last, m1, True)

    def attend1(c, carry):
        attend(1, c, m1)
        return carry

    lax.fori_loop(0, last + 1, attend1, 0)

    w0 = pl.multiple_of(jnp.maximum(t0 - WINDOW, 0), tq)
    wlen = WINDOW + tq
    wpos = w0 + lax.broadcasted_iota(jnp.int32, (wlen, cols), 0)
    in_window = (wpos <= t_cols) & (wpos > t_cols - WINDOW)
    k_win = kw_ref[0, pl.ds(w0, wlen), :]
    s_win = [jnp.where(in_window, _dot(k_win, qsel_ref[g, 0:LANES, :]), MASK_VALUE)
             for g in range(NSA_GROUPS)]
    acc_win = [_dot(vwt_ref[0, g, :, pl.ds(w0, wlen)],
                    jnp.exp2(s_win[g] - jnp.max(s_win[g], axis=0, keepdims=True)).astype(BF16))
               for g in range(NSA_GROUPS)]

    gates = gt_ref[0]
    for g in range(NSA_GROUPS):
        ones_row = (1 - g) * NSA_DH
        acc_s = acc_ref[g]
        o_s = acc_s * (1.0 / jnp.maximum(acc_s[ones_row:ones_row + 1], 1e-30))
        o_w = acc_win[g] * (1.0 / jnp.maximum(acc_win[g][ones_row:ones_row + 1], 1e-30))

        o_c = oc_ref[g]
        heads = []
        for h in range(NSA_HPG):
            c = slice(h * tq, (h + 1) * tq)
            gi = (NSA_HPG * g + h) * 3
            mix = (gates[gi:gi + 1] * o_c[:, c] + gates[gi + 1:gi + 2] * o_s[:, c]
                   + gates[gi + 2:gi + 3] * o_w[:, c])
            heads.append(mix[g * NSA_DH:(g + 1) * NSA_DH])
        for j in range(NSA_HPG // 2):
            pair = jnp.concatenate([heads[2 * j], heads[2 * j + 1]], axis=0)
            c0 = (NSA_HPG * g + 2 * j) * NSA_DH
            o_ref[0, :, c0:c0 + LANES] = pair.T.astype(BF16)


def _nsa_attention(qt, kc, vct, ks, vst, kw, vwt, gt, selt):
    B, _, _, S = qt.shape
    tq = ATT_TQ
    cols = NSA_HPG * tq
    seq_rows = lambda n: pl.BlockSpec((1, S, n), lambda b, i: (b, 0, 0))
    seq_cols = pl.BlockSpec((1, NSA_GROUPS, LANES, S), lambda b, i: (b, 0, 0, 0))
    whole = lambda a: pl.BlockSpec((1,) + a.shape[1:], lambda b, i: (b, 0, 0))
    return pl.pallas_call(
        functools.partial(_nsa_kernel, seq_len=S),
        grid=(B, S // tq),
        in_specs=[pl.BlockSpec((1, NSA_HEADS, LANES, tq), lambda b, i: (b, 0, 0, i)),
                  whole(kc), whole(vct), seq_rows(2 * LANES), seq_cols, seq_rows(LANES),
                  seq_cols, pl.BlockSpec((1, GATE_ROWS, tq), lambda b, i: (b, 0, i)),
                  pl.BlockSpec(selt.shape, lambda b, i: (0, 0))],
        out_specs=pl.BlockSpec((1, tq, NSA_WIDTH), lambda b, i: (b, i, 0)),
        out_shape=jax.ShapeDtypeStruct((B, S, NSA_WIDTH), BF16),
        scratch_shapes=[pltpu.VMEM((NSA_GROUPS, 2 * LANES, cols), BF16),
                        pltpu.VMEM((NSA_GROUPS, LANES, cols), F32),
                        pltpu.VMEM((NSA_GROUPS, S, cols), F32),
                        pltpu.VMEM((NSA_GROUPS, LANES, cols), F32)],
        compiler_params=_params("parallel", "parallel"),
        name="nsa_attention",
    )(qt, kc, vct, ks, vst, kw, vwt, gt, selt)


def _norm_mlp_norm(x, mix, g1_ref, b1_ref, w1_ref, w2_ref, g2_ref, b2_ref, x1_ref, xb_ref, acc_ref):
    x1_ref[...] = _layer_norm(ALPHA * x + mix, g1_ref[...], b1_ref[...])
    xb_ref[...] = x1_ref[...].astype(BF16)
    for c in range(D_FF // FF_CHUNK):
        cols = slice(c * FF_CHUNK, (c + 1) * FF_CHUNK)
        h = jnp.maximum(_dot(xb_ref[...], w1_ref[:, cols]), 0.0)
        part = _dot((h * h).astype(BF16), w2_ref[cols, :])
        if c == 0:
            acc_ref[...] = part
        else:
            acc_ref[...] += part
    return _layer_norm(ALPHA * x1_ref[...] + acc_ref[...], g2_ref[...], b2_ref[...])


def _post_scratch(tm):
    return [pltpu.VMEM((tm, D_MODEL), F32), pltpu.VMEM((tm, D_MODEL), BF16), pltpu.VMEM((tm, D_MODEL), F32)]


def _post_even_kernel(o_ref, u_ref, x_ref, wo_ref, pw_ref, ps_ref, g1_ref, b1_ref, w1_ref, w2_ref,
                      g2_ref, b2_ref, y_ref, ext_ref, cat_ref, x1_ref, xb_ref, acc_ref):
    tm = u_ref.shape[1]
    i = pl.program_id(1)

    @pl.when(i == 0)
    def _():
        ext_ref[0:POOL_HALO, :] = jnp.zeros((POOL_HALO, POOL_WIDTH), F32)

    u = u_ref[0]
    ext_ref[POOL_HALO:, :] = u
    t = (i * tm + lax.broadcasted_iota(jnp.int32, (tm, 1), 0) + 1).astype(F32)
    cat_ref[:, 0:NSA_WIDTH] = o_ref[0]
    for gi, w in enumerate(POOL_WINDOWS):
        cols = slice(gi * POOL_GROUP_DIM, (gi + 1) * POOL_GROUP_DIM)
        u_g = u[:, cols]
        acc = u_g
        for d in range(1, w):
            acc = acc + ext_ref[POOL_HALO - d:POOL_HALO - d + tm, cols]
        r = acc / jnp.minimum(t, float(w)) - u_g
        y_g = _dot(r.astype(BF16), pw_ref[gi]) * ps_ref[:, cols]
        cat_ref[:, NSA_WIDTH + gi * POOL_GROUP_DIM:NSA_WIDTH + (gi + 1) * POOL_GROUP_DIM] = y_g.astype(BF16)
    ext_ref[0:POOL_HALO, :] = u[tm - POOL_HALO:, :]
    mix = _dot(cat_ref[...], wo_ref[...])
    y_ref[0] = _norm_mlp_norm(x_ref[0], mix, g1_ref, b1_ref, w1_ref, w2_ref, g2_ref, b2_ref,
                              x1_ref, xb_ref, acc_ref)


def _post_even(o, u, x, wo, pw, ps, g1, b1, w1, w2, g2, b2):
    B, S, D = x.shape
    tm = PROJ_TM
    row = lambda n: pl.BlockSpec((1, tm, n), lambda bb, i: (bb, i, 0))
    full = lambda a: pl.BlockSpec(a.shape, lambda bb, i: (0,) * a.ndim)
    return pl.pallas_call(
        _post_even_kernel,
        grid=(B, S // tm),
        in_specs=[row(NSA_WIDTH), row(POOL_WIDTH), row(D), full(wo), full(pw), full(ps), full(g1), full(b1),
                  full(w1), full(w2), full(g2), full(b2)],
        out_specs=row(D),
        out_shape=jax.ShapeDtypeStruct((B, S, D), F32),
        scratch_shapes=[pltpu.VMEM((POOL_HALO + tm, POOL_WIDTH), F32),
                        pltpu.VMEM((tm, NSA_WIDTH + POOL_WIDTH), BF16)] + _post_scratch(tm),
        compiler_params=_params("parallel", "arbitrary"),
        name="post_even",
    )(o, u, x, wo, pw, ps, g1, b1, w1, w2, g2, b2)


def _proj_odd_kernel(x_ref, wq_ref, wk_ref, wv_ref, wr_ref, wa_ref, gw_ref, gb_ref,
                     q_ref, k_ref, v_ref, r_ref, la_ref):
    xb = x_ref[0].astype(BF16)
    n = FF_CHUNK
    for c in range(GLA_KEY_WIDTH // n):
        q_ref[0, :, c * n:(c + 1) * n] = _dot(xb, wq_ref[:, c * n:(c + 1) * n])
        k_ref[0, :, c * n:(c + 1) * n] = _dot(xb, wk_ref[:, c * n:(c + 1) * n])
    for c in range(GLA_VAL_WIDTH // n):
        v_ref[0, :, c * n:(c + 1) * n] = _dot(xb, wv_ref[:, c * n:(c + 1) * n]).astype(BF16)
        r_ref[0, :, c * n:(c + 1) * n] = _dot(xb, wr_ref[:, c * n:(c + 1) * n])
    a = _dot(xb, wa_ref[...])
    z = _dot(a.astype(BF16), gw_ref[...]) + gb_ref[...]
    log_sig = jnp.minimum(z, 0.0) - jnp.log(1.0 + jnp.exp(-jnp.abs(z)))
    la_ref[0] = log_sig / GLA_TAU


def _proj_odd(x, wq, wk, wv, wr, wa, gw, gb):
    B, S, D = x.shape
    tm = PROJ_TM
    row = lambda n: pl.BlockSpec((1, tm, n), lambda b, i: (b, i, 0))
    full = lambda a: pl.BlockSpec(a.shape, lambda b, i: (0,) * a.ndim)
    return pl.pallas_call(
        _proj_odd_kernel,
        grid=(B, S // tm),
        in_specs=[row(D), full(wq), full(wk), full(wv), full(wr), full(wa), full(gw), full(gb)],
        out_specs=(row(GLA_KEY_WIDTH), row(GLA_KEY_WIDTH), row(GLA_VAL_WIDTH), row(GLA_VAL_WIDTH),
                   row(GLA_KEY_WIDTH)),
        out_shape=(jax.ShapeDtypeStruct((B, S, GLA_KEY_WIDTH), F32),
                   jax.ShapeDtypeStruct((B, S, GLA_KEY_WIDTH), F32),
                   jax.ShapeDtypeStruct((B, S, GLA_VAL_WIDTH), BF16),
                   jax.ShapeDtypeStruct((B, S, GLA_VAL_WIDTH), F32),
                   jax.ShapeDtypeStruct((B, S, GLA_KEY_WIDTH), F32)),
        compiler_params=_params("parallel", "parallel"),
        name="proj_odd",
    )(x, wq, wk, wv, wr, wa, gw, gb)


def _gla_kernel(q_ref, k_ref, la_ref, v_ref, r_ref, ng_ref, o_ref, state_ref):
    C = GLA_CHUNK

    @pl.when(pl.program_id(1) == 0)
    def _():
        state_ref[...] = jnp.zeros(state_ref.shape, F32)

    row = lax.broadcasted_iota(jnp.int32, (C, GLA_DK), 0)
    causal = (lax.broadcasted_iota(jnp.int32, (C, C), 0) >= lax.broadcasted_iota(jnp.int32, (C, C), 1))
    for n in range(q_ref.shape[1] // C):
        rows = slice(n * C, (n + 1) * C)
        for h in range(GLA_HEADS):
            kcols = slice(h * GLA_DK, (h + 1) * GLA_DK)
            vcols = slice(h * GLA_DV, (h + 1) * GLA_DV)
            b = la_ref[0, rows, kcols]
            shift = 1
            while shift < C:
                b = b + jnp.where(row >= shift, pltpu.roll(b, shift, axis=0), 0.0)
                shift *= 2
            b_last = b[C - 1:C, :]
            q_t = (q_ref[0, rows, kcols] * (GLA_DK ** -0.5)) * jnp.exp(b)
            k = k_ref[0, rows, kcols]
            k_t = k * jnp.exp(-b)
            k_u = k * jnp.exp(b_last - b)
            v = v_ref[0, rows, vcols]
            q_tb = q_t.astype(BF16)
            att = jnp.where(causal, _dot_nt(q_tb, k_t.astype(BF16)), 0.0)
            state = state_ref[h]
            o = _dot(att.astype(BF16), v) + _dot_nt(q_tb, state.astype(BF16))
            upd = _dot_tn(v, k_u.astype(BF16))
            state_ref[h] = jnp.exp(b_last) * state + upd
            o = o * lax.rsqrt(jnp.mean(o * o, axis=-1, keepdims=True) + LN_EPS) * ng_ref[...]
            o_ref[0, rows, vcols] = (o * jax.nn.silu(r_ref[0, rows, vcols])).astype(BF16)


def _gla(q, k, la, v, r, ng):
    B, S, _ = q.shape
    row = lambda n: pl.BlockSpec((1, GLA_TILE, n), lambda b, i: (b, i, 0))
    return pl.pallas_call(
        _gla_kernel,
        grid=(B, S // GLA_TILE),
        in_specs=[row(GLA_KEY_WIDTH), row(GLA_KEY_WIDTH), row(GLA_KEY_WIDTH), row(GLA_VAL_WIDTH),
                  row(GLA_VAL_WIDTH), pl.BlockSpec(ng.shape, lambda b, i: (0, 0))],
        out_specs=row(GLA_VAL_WIDTH),
        out_shape=jax.ShapeDtypeStruct((B, S, GLA_VAL_WIDTH), BF16),
        scratch_shapes=[pltpu.VMEM((GLA_HEADS, GLA_DV, GLA_DK), F32)],
        compiler_params=_params("parallel", "arbitrary"),
        name="gla",
    )(q, k, la, v, r, ng)


def _post_odd_kernel(o_ref, x_ref, wo_ref, g1_ref, b1_ref, w1_ref, w2_ref, g2_ref, b2_ref, y_ref,
                     x1_ref, xb_ref, acc_ref):
    mix = _dot(o_ref[0], wo_ref[...])
    y_ref[0] = _norm_mlp_norm(x_ref[0], mix, g1_ref, b1_ref, w1_ref, w2_ref, g2_ref, b2_ref,
                              x1_ref, xb_ref, acc_ref)


def _post_odd(o, x, wo, g1, b1, w1, w2, g2, b2):
    B, S, D = x.shape
    tm = PROJ_TM
    row = lambda n: pl.BlockSpec((1, tm, n), lambda bb, i: (bb, i, 0))
    full = lambda a: pl.BlockSpec(a.shape, lambda bb, i: (0,) * a.ndim)
    return pl.pallas_call(
        _post_odd_kernel,
        grid=(B, S // tm),
        in_specs=[row(GLA_VAL_WIDTH), row(D), full(wo), full(g1), full(b1), full(w1), full(w2), full(g2),
                  full(b2)],
        out_specs=row(D),
        out_shape=jax.ShapeDtypeStruct((B, S, D), F32),
        scratch_shapes=_post_scratch(tm),
        compiler_params=_params("parallel", "parallel"),
        name="post_odd",
    )(o, x, wo, g1, b1, w1, w2, g2, b2)


def _even_weights(w_in, cmp_pos, cmp_w1, cmp_b1, cmp_w2, pool_w, pool_scale, w_out):
    o1 = NSA_WIDTH
    o2 = o1 + 6 * NSA_KV_WIDTH
    o3 = o2 + 3 * NSA_HEADS
    D = w_in.shape[0]
    wq = (w_in[:, :o1] * (NSA_DH ** -0.5 * LOG2_E)).reshape(D, NSA_GROUPS, NSA_HPG, NSA_DH)
    slots = jnp.zeros((D, NSA_GROUPS, NSA_HPG, NSA_GROUPS, NSA_DH), F32)
    for g in range(NSA_GROUPS):
        slots = slots.at[:, g, :, g, :].set(wq[:, g])
    wq = slots.reshape(D, NSA_HEADS * LANES)
    kv = [w_in[:, o1 + n * NSA_KV_WIDTH:o1 + (n + 1) * NSA_KV_WIDTH] for n in range(6)]
    k_cmp, v_cmp, k_slc, v_slc, k_win, v_win = kv
    wg = jnp.pad(w_in[:, o2:o3], ((0, 0), (0, GATE_ROWS - 3 * NSA_HEADS)))
    wt = jnp.concatenate([wq, v_slc, v_win, wg], axis=1).T.astype(BF16)
    wr = jnp.concatenate([k_cmp, v_cmp, k_slc, k_win, w_in[:, o3:]], axis=1).astype(BF16)
    half = CMP_BLOCK // 2
    eye = jnp.eye(NSA_GROUPS, dtype=F32)
    w1 = cmp_w1.reshape(2, 2, half, NSA_DH, NSA_DH)
    w1 = jnp.einsum('whldo,gk->whlgdko', w1, eye).reshape(2, 2, half * LANES, LANES).astype(BF16)
    pos = cmp_pos.reshape(2, 2, half, 1, NSA_DH)
    pos = jnp.broadcast_to(pos, (2, 2, half, NSA_GROUPS, NSA_DH)).reshape(2, 2, 1, half * LANES)
    b1 = jnp.tile(cmp_b1, (1, NSA_GROUPS)).reshape(2, 1, LANES)
    w2 = jnp.einsum('wdo,gk->wgdko', cmp_w2, eye).reshape(2, LANES, LANES).astype(BF16)
    return dict(wt=wt, wr=wr, pos=pos, w1=w1, b1=b1, w2=w2,
                pw=pool_w.astype(BF16), ps=pool_scale.reshape(1, POOL_WIDTH), wo=w_out.astype(BF16))


def _selection_constants(seq_len):
    n_cmp = (seq_len - CMP_BLOCK) // CMP_STRIDE + 1
    n_sel = seq_len // SEL_BLOCK
    sub = np.arange(n_cmp)[:, None] + np.arange(CMP_BLOCK // CMP_STRIDE)[None, :]
    owner = sub // (SEL_BLOCK // CMP_STRIDE)
    sel_map = (owner[:, :, None] == np.arange(n_sel)[None, None, :]).sum(1).astype(np.float32)
    selt = np.zeros((n_sel, seq_len // CMP_STRIDE), np.float32)
    selt[:, :n_cmp] = sel_map.T
    eb = np.zeros((seq_len, LANES), np.float32)
    eb[np.arange(seq_len), np.arange(seq_len) // SEL_BLOCK] = MASK_VALUE
    return jnp.asarray(selt, BF16), jnp.asarray(eb, BF16)


def _even_layer(x, w, mlp, selt, eb):
    B, S, _ = x.shape
    qt, vst, vwt, gt, kc, vc, ks, kw, u = _proj_even(x, w['wt'], w['wr'], eb)
    nr = S // CMP_STRIDE
    kcc, vcct = _compress(kc.reshape(B, nr, CMP_STRIDE * LANES), vc.reshape(B, nr, CMP_STRIDE * LANES),
                          w['pos'], w['w1'], w['b1'], w['w2'])
    o = _nsa_attention(qt, kcc, vcct, ks, vst, kw, vwt, gt, selt)
    return _post_even(o, u, x, w['wo'], w['pw'], w['ps'], *mlp)


def _odd_layer(x, w_in, gate_w2, gate_b, norm_g, w_out, mlp):
    o1 = GLA_KEY_WIDTH
    o2 = o1 + GLA_KEY_WIDTH
    o3 = o2 + GLA_VAL_WIDTH
    o4 = o3 + GLA_VAL_WIDTH
    wb = w_in.astype(BF16)
    wa = jnp.pad(wb[:, o4:], ((0, 0), (0, LANES - GLA_RANK)))
    gw = jnp.pad(gate_w2.astype(BF16), ((0, LANES - GLA_RANK), (0, 0)))
    q, k, v, r, la = _proj_odd(x, wb[:, :o1], wb[:, o1:o2], wb[:, o2:o3], wb[:, o3:o4], wa, gw,
                               gate_b.reshape(1, GLA_KEY_WIDTH))
    o = _gla(q, k, la, v, r, norm_g.reshape(1, GLA_DV))
    return _post_odd(o, x, w_out.astype(BF16), *mlp)


def kernel(x, a_w_in, a_cmp_pos, a_cmp_w1, a_cmp_b1, a_cmp_w2, a_pool_w, a_pool_scale, a_w_out, c_w_in, c_gate_w2, c_gate_b, c_norm_g, c_w_out, ln1_g, ln1_b, ln2_g, ln2_b, mlp_w1, mlp_w2):
    S = x.shape[1]
    selt, eb = _selection_constants(S)
    for i in range(DEPTH):
        j = i // 2
        mlp = (ln1_g[i].reshape(1, D_MODEL), ln1_b[i].reshape(1, D_MODEL),
               mlp_w1[i].astype(BF16), mlp_w2[i].astype(BF16),
               ln2_g[i].reshape(1, D_MODEL), ln2_b[i].reshape(1, D_MODEL))
        if i % 2 == 0:
            w = _even_weights(a_w_in[j], a_cmp_pos[j], a_cmp_w1[j], a_cmp_b1[j], a_cmp_w2[j],
                              a_pool_w[j], a_pool_scale[j], a_w_out[j])
            x = _even_layer(x, w, mlp, selt, eb)
        else:
            x = _odd_layer(x, c_w_in[j], c_gate_w2[j], c_gate_b[j], c_norm_g[j], c_w_out[j], mlp)
    return x
```

```python
import functools

import numpy as np
import jax
import jax.numpy as jnp
from jax import lax
from jax.experimental import pallas as pl
from jax.experimental.pallas import tpu as pltpu

F32 = jnp.float32
BF16 = jnp.bfloat16

D_MODEL = 1024
DEPTH = 4
NSA_HEADS = 8
NSA_GROUPS = 2
NSA_HPG = NSA_HEADS // NSA_GROUPS
NSA_DH = 64
NSA_WIDTH = NSA_HEADS * NSA_DH
NSA_KV_WIDTH = NSA_GROUPS * NSA_DH
CMP_BLOCK = 32
CMP_STRIDE = 16
SEL_BLOCK = 64
SEL_TOP_N = 16
WINDOW = 512
POOL_WINDOWS = (2, 4, 8, 16)
POOL_GROUP_DIM = 128
POOL_WIDTH = 512
GLA_HEADS = 4
GLA_DK = 128
GLA_DV = 256
GLA_KEY_WIDTH = GLA_HEADS * GLA_DK
GLA_VAL_WIDTH = GLA_HEADS * GLA_DV
GLA_RANK = 16
GLA_TAU = 16.0
GLA_CHUNK = 64
D_FF = 4 * D_MODEL
ALPHA = (2 * DEPTH) ** 0.25
LN_EPS = 1e-5
MASK_VALUE = -1e30
FORCE_SCORE = 1e4
LOG2_E = 1.4426950408889634

LANES = 128
VMEM_LIMIT_BYTES = 56 * 1024 * 1024

PROJ_TM = 512
ATT_TQ = 256
ATT_CK = 512
FF_CHUNK = 512
POOL_HALO = 16
GLA_TILE = 512
GATE_ROWS = 32
PV_ROWS = 80
PROJ_T_ROWS = NSA_HEADS * LANES + 2 * LANES + GATE_ROWS


def _dot(a, b):
    return jnp.dot(a, b, preferred_element_type=F32)


def _dot_nt(a, b):
    return lax.dot_general(a, b, (((1,), (1,)), ((), ())), preferred_element_type=F32)


def _dot_tn(a, b):
    return lax.dot_general(a, b, (((0,), (0,)), ((), ())), preferred_element_type=F32)


def _params(*semantics):
    return pltpu.CompilerParams(dimension_semantics=semantics, vmem_limit_bytes=VMEM_LIMIT_BYTES)


def _exp2_bf16(d):
    return jnp.exp2(d.astype(BF16))


def _layer_norm(z, g, b):
    mu = jnp.mean(z, axis=-1, keepdims=True)
    zc = z - mu
    var = jnp.mean(zc * zc, axis=-1, keepdims=True)
    return zc * lax.rsqrt(var + LN_EPS) * g + b


def _proj_even_kernel(x_ref, wt_ref, wr_ref, eb_ref,
                      qt_ref, vst_ref, vwt_ref, gt_ref, kc_ref, vc_ref, ks_ref, kw_ref, u_ref, cmp_ref):
    xb = x_ref[0].astype(BF16)
    ht = _dot_nt(wt_ref[...], xb)
    nq = NSA_HEADS * LANES
    for h in range(NSA_HEADS):
        qt_ref[0, h] = ht[h * LANES:(h + 1) * LANES].astype(BF16)
    pad_rows = lax.broadcasted_iota(jnp.int32, (PV_ROWS - NSA_DH, ht.shape[1]), 0)
    ones_then_zeros = jnp.where(pad_rows == 0, 1.0, 0.0)
    for g in range(NSA_GROUPS):
        for ref, base in ((vst_ref, nq), (vwt_ref, nq + LANES)):
            v_g = ht[base + g * NSA_DH:base + (g + 1) * NSA_DH]
            ref[0, g] = jnp.concatenate([v_g, ones_then_zeros], axis=0).astype(BF16)
    gt_ref[0] = jax.nn.sigmoid(ht[nq + 2 * LANES:])
    hr = _dot(xb, wr_ref[...])
    n_rows = hr.shape[0] // CMP_STRIDE
    for which, dst in enumerate((kc_ref, vc_ref)):
        cmp_ref[which] = hr[:, which * LANES:(which + 1) * LANES]
        for l in range(CMP_STRIDE):
            dst[0, :, l * LANES:(l + 1) * LANES] = cmp_ref[which, pl.ds(l, n_rows, stride=CMP_STRIDE), :]
    ks_ref[0, :, :LANES] = hr[:, 2 * LANES:3 * LANES].astype(BF16)
    ks_ref[0, :, LANES:] = eb_ref[...]
    kw_ref[0] = hr[:, 3 * LANES:4 * LANES].astype(BF16)
    u_ref[0] = hr[:, 4 * LANES:]


def _proj_even(x, wt, wr, eb):
    B, S, D = x.shape
    tm = PROJ_TM
    row = lambda n: pl.BlockSpec((1, tm, n), lambda b, i: (b, i, 0))
    col = lambda n: pl.BlockSpec((1, n, tm), lambda b, i: (b, 0, i))
    full = lambda a: pl.BlockSpec(a.shape, lambda b, i: (0,) * a.ndim)
    out_shape = (
        jax.ShapeDtypeStruct((B, NSA_HEADS, LANES, S), BF16),
        jax.ShapeDtypeStruct((B, NSA_GROUPS, PV_ROWS, S), BF16),
        jax.ShapeDtypeStruct((B, NSA_GROUPS, PV_ROWS, S), BF16),
        jax.ShapeDtypeStruct((B, GATE_ROWS, S), F32),
        jax.ShapeDtypeStruct((B, S // CMP_STRIDE, CMP_STRIDE * LANES), F32),
        jax.ShapeDtypeStruct((B, S // CMP_STRIDE, CMP_STRIDE * LANES), F32),
        jax.ShapeDtypeStruct((B, S, 2 * LANES), BF16),
        jax.ShapeDtypeStruct((B, S, LANES), BF16),
        jax.ShapeDtypeStruct((B, S, POOL_WIDTH), F32),
    )
    out_specs = (
        pl.BlockSpec((1, NSA_HEADS, LANES, tm), lambda b, i: (b, 0, 0, i)),
        pl.BlockSpec((1, NSA_GROUPS, PV_ROWS, tm), lambda b, i: (b, 0, 0, i)),
        pl.BlockSpec((1, NSA_GROUPS, PV_ROWS, tm), lambda b, i: (b, 0, 0, i)), col(GATE_ROWS),
        pl.BlockSpec((1, tm // CMP_STRIDE, CMP_STRIDE * LANES), lambda b, i: (b, i, 0)),
        pl.BlockSpec((1, tm // CMP_STRIDE, CMP_STRIDE * LANES), lambda b, i: (b, i, 0)),
        row(2 * LANES), row(LANES), row(POOL_WIDTH),
    )
    return pl.pallas_call(
        _proj_even_kernel,
        grid=(B, S // tm),
        in_specs=[row(D), full(wt), full(wr), pl.BlockSpec((tm, LANES), lambda b, i: (i, 0))],
        out_specs=out_specs,
        out_shape=out_shape,
        scratch_shapes=[pltpu.VMEM((2, tm, LANES), F32)],
        compiler_params=_params("parallel", "parallel"),
        name="proj_even",
    )(x, wt, wr, eb)


def _gelu_tanh(x):
    return 0.5 * x * (1.0 + jnp.tanh(0.7978845608028654 * (x + 0.044715 * (x * x * x))))


def _compress_kernel(kc_ref, vc_ref, pos_ref, w1_ref, b1_ref, w2_ref, okc_ref, ovct_ref):
    for which, src in enumerate((kc_ref, vc_ref)):
        xr = src[0]
        nr = xr.shape[0]
        lo = _dot((xr + pos_ref[which, 0]).astype(BF16), w1_ref[which, 0])
        hi = _dot((xr + pos_ref[which, 1]).astype(BF16), w1_ref[which, 1])
        hi_next = pltpu.roll(hi, nr - 1, axis=0)
        act = _gelu_tanh(lo + hi_next + b1_ref[which])
        out = _dot(act.astype(BF16), w2_ref[which])
        rows = lax.broadcasted_iota(jnp.int32, out.shape, 0)
        out = jnp.where(rows < nr - 1, out, 0.0)
        if which == 0:
            okc_ref[0] = out.astype(BF16)
        else:
            ovct_ref[0] = out.T.astype(BF16)


def _compress(kc, vc, pos, w1, b1, w2):
    B, NR, W = kc.shape
    blk = pl.BlockSpec((1, NR, W), lambda b: (b, 0, 0))
    full = lambda a: pl.BlockSpec(a.shape, lambda b: (0,) * a.ndim)
    return pl.pallas_call(
        _compress_kernel,
        grid=(B,),
        in_specs=[blk, blk, full(pos), full(w1), full(b1), full(w2)],
        out_specs=(pl.BlockSpec((1, NR, LANES), lambda b: (b, 0, 0)),
                   pl.BlockSpec((1, LANES, NR), lambda b: (b, 0, 0))),
        out_shape=(jax.ShapeDtypeStruct((B, NR, LANES), BF16),
                   jax.ShapeDtypeStruct((B, LANES, NR), BF16)),
        compiler_params=_params("parallel"),
        name="compress",
    )(kc, vc, pos, w1, b1, w2)


def _nsa_kernel(qt_ref, kc_ref, vct_ref, ks_ref, vst_ref, kw_ref, vwt_ref, gt_ref, selt_ref, o_ref,
                qsel_ref, oc_ref, s_ref, acc_ref, *, seq_len):
    tq = ATT_TQ
    cols = NSA_HPG * tq
    n_sel = seq_len // SEL_BLOCK
    t0 = pl.program_id(1) * tq
    t_q = t0 + lax.broadcasted_iota(jnp.int32, (1, tq), 1)
    t_cols = jnp.concatenate([t_q] * NSA_HPG, axis=1)

    for g in range(NSA_GROUPS):
        qt = jnp.concatenate([qt_ref[0, NSA_HPG * g + h] for h in range(NSA_HPG)], axis=1)

        s_c = _dot(kc_ref[0], qt)
        c_end = lax.broadcasted_iota(jnp.int32, s_c.shape, 0) * CMP_STRIDE + (CMP_BLOCK - 1)
        valid_c = c_end <= t_cols
        s_c = jnp.where(valid_c, s_c, MASK_VALUE)
        e_c = jnp.where(valid_c, jnp.exp2(s_c - jnp.max(s_c, axis=0, keepdims=True)), 0.0)
        p_c = e_c * (1.0 / jnp.maximum(jnp.sum(e_c, axis=0, keepdims=True), 1e-30))
        oc_ref[g] = _dot(vct_ref[0], p_c.astype(BF16))[g * NSA_DH:(g + 1) * NSA_DH]

        p_sum = p_c[:, 0:tq]
        for h in range(1, NSA_HPG):
            p_sum = p_sum + p_c[:, h * tq:(h + 1) * tq]
        p_hi = p_sum.astype(BF16)
        p_lo = (p_sum - p_hi.astype(F32)).astype(BF16)
        imp = _dot(selt_ref[...], p_hi) + _dot(selt_ref[...], p_lo)
        blk = lax.broadcasted_iota(jnp.int32, imp.shape, 0)
        cur = t_q >> 6
        forced = (blk == 0) | (blk == cur) | (blk == cur - 1)
        score = jnp.where(forced, FORCE_SCORE, jnp.where(blk > cur, -FORCE_SCORE, imp))
        beaten = jnp.zeros(imp.shape, F32)
        for i in range(n_sel):
            s_i = score[i:i + 1, :]
            wins = (s_i > score) | ((s_i == score) & (blk > i))
            beaten = beaten + jnp.where(wins, 1.0, 0.0)
        not_sel = jnp.where(beaten < float(SEL_TOP_N), 0.0, 1.0)
        not_sel = jnp.concatenate([not_sel, jnp.zeros((LANES - n_sel, tq), F32)], axis=0).astype(BF16)
        qsel_ref[g, 0:LANES, :] = qt
        qsel_ref[g, LANES:, :] = jnp.concatenate([not_sel] * NSA_HPG, axis=1)

    def score(c, maxes, on_diagonal):
        k0 = pl.multiple_of(c * ATT_CK, ATT_CK)
        k_blk = ks_ref[0, pl.ds(k0, ATT_CK), :]
        out = []
        for g in range(NSA_GROUPS):
            s = _dot(k_blk, qsel_ref[g])
            if on_diagonal:
                kpos = k0 + lax.broadcasted_iota(jnp.int32, s.shape, 0)
                s = jnp.where(kpos <= t_cols, s, MASK_VALUE)
            s_ref[g, pl.ds(k0, ATT_CK), :] = s
            out.append(jnp.maximum(maxes[g], jnp.max(s, axis=0, keepdims=True)))
        return tuple(out)

    last = (t0 + tq - 1) // ATT_CK
    m_init = jnp.full((1, cols), MASK_VALUE, F32)
    maxes = lax.fori_loop(0, last, lambda c, m: score(c, m, False), (m_init,) * NSA_GROUPS)
    maxes = score(last, maxes, True)

    acc_ref[...] = jnp.zeros(acc_ref.shape, F32)

    def attend(c, carry):
        k0 = pl.multiple_of(c * ATT_CK, ATT_CK)
        for g in range(NSA_GROUPS):
            p = _exp2_bf16(s_ref[g, pl.ds(k0, ATT_CK), :] - maxes[g])
            acc_ref[g] += _dot(vst_ref[0, g, :, pl.ds(k0, ATT_CK)], p)
        return carry

    lax.fori_loop(0, last + 1, attend, 0)

    w0 = pl.multiple_of(jnp.maximum(t0 - WINDOW, 0), tq)
    wlen = WINDOW + tq
    wpos = w0 + lax.broadcasted_iota(jnp.int32, (wlen, cols), 0)
    in_window = (wpos <= t_cols) & (wpos > t_cols - WINDOW)
    k_win = kw_ref[0, pl.ds(w0, wlen), :]
    s_win = [jnp.where(in_window, _dot(k_win, qsel_ref[g, 0:LANES, :]), MASK_VALUE)
             for g in range(NSA_GROUPS)]
    acc_win = [_dot(vwt_ref[0, g, :, pl.ds(w0, wlen)],
                    _exp2_bf16(s_win[g] - jnp.max(s_win[g], axis=0, keepdims=True)))
               for g in range(NSA_GROUPS)]

    gates = gt_ref[0]
    for g in range(NSA_GROUPS):
        acc_s = acc_ref[g]
        o_s = acc_s[0:NSA_DH] * (1.0 / jnp.maximum(acc_s[NSA_DH:NSA_DH + 1], 1e-30))
        o_w = acc_win[g][0:NSA_DH] * (1.0 / jnp.maximum(acc_win[g][NSA_DH:NSA_DH + 1], 1e-30))

        o_c = oc_ref[g]
        heads = []
        for h in range(NSA_HPG):
            c = slice(h * tq, (h + 1) * tq)
            gi = (NSA_HPG * g + h) * 3
            mix = (gates[gi:gi + 1] * o_c[:, c] + gates[gi + 1:gi + 2] * o_s[:, c]
                   + gates[gi + 2:gi + 3] * o_w[:, c])
            heads.append(mix)
        for j in range(NSA_HPG // 2):
            pair = jnp.concatenate([heads[2 * j], heads[2 * j + 1]], axis=0)
            c0 = (NSA_HPG * g + 2 * j) * NSA_DH
            o_ref[0, :, c0:c0 + LANES] = pair.T.astype(BF16)


def _nsa_attention(qt, kc, vct, ks, vst, kw, vwt, gt, selt):
    B, _, _, S = qt.shape
    tq = ATT_TQ
    cols = NSA_HPG * tq
    seq_rows = lambda n: pl.BlockSpec((1, S, n), lambda b, i: (b, 0, 0))
    seq_cols = pl.BlockSpec((1, NSA_GROUPS, PV_ROWS, S), lambda b, i: (b, 0, 0, 0))
    whole = lambda a: pl.BlockSpec((1,) + a.shape[1:], lambda b, i: (b, 0, 0))
    return pl.pallas_call(
        functools.partial(_nsa_kernel, seq_len=S),
        grid=(B, S // tq),
        in_specs=[pl.BlockSpec((1, NSA_HEADS, LANES, tq), lambda b, i: (b, 0, 0, i)),
                  whole(kc), whole(vct), seq_rows(2 * LANES), seq_cols, seq_rows(LANES),
                  seq_cols, pl.BlockSpec((1, GATE_ROWS, tq), lambda b, i: (b, 0, i)),
                  pl.BlockSpec(selt.shape, lambda b, i: (0, 0))],
        out_specs=pl.BlockSpec((1, tq, NSA_WIDTH), lambda b, i: (b, i, 0)),
        out_shape=jax.ShapeDtypeStruct((B, S, NSA_WIDTH), BF16),
        scratch_shapes=[pltpu.VMEM((NSA_GROUPS, 2 * LANES, cols), BF16),
                        pltpu.VMEM((NSA_GROUPS, NSA_DH, cols), F32),
                        pltpu.VMEM((NSA_GROUPS, S, cols), F32),
                        pltpu.VMEM((NSA_GROUPS, PV_ROWS, cols), F32)],
        compiler_params=_params("parallel", "parallel"),
        name="nsa_attention",
    )(qt, kc, vct, ks, vst, kw, vwt, gt, selt)


def _norm_mlp_norm(x, mix, g1_ref, b1_ref, w1_ref, w2_ref, g2_ref, b2_ref, x1_ref, xb_ref, acc_ref):
    x1_ref[...] = _layer_norm(ALPHA * x + mix, g1_ref[...], b1_ref[...])
    xb_ref[...] = x1_ref[...].astype(BF16)
    for c in range(D_FF // FF_CHUNK):
        cols = slice(c * FF_CHUNK, (c + 1) * FF_CHUNK)
        h = jnp.maximum(_dot(xb_ref[...], w1_ref[:, cols]), 0.0)
        part = _dot((h * h).astype(BF16), w2_ref[cols, :])
        if c == 0:
            acc_ref[...] = part
        else:
            acc_ref[...] += part
    return _layer_norm(ALPHA * x1_ref[...] + acc_ref[...], g2_ref[...], b2_ref[...])


def _post_scratch(tm):
    return [pltpu.VMEM((tm, D_MODEL), F32), pltpu.VMEM((tm, D_MODEL), BF16), pltpu.VMEM((tm, D_MODEL), F32)]


def _post_even_kernel(o_ref, u_ref, x_ref, wo_ref, pw_ref, ps_ref, g1_ref, b1_ref, w1_ref, w2_ref,
                      g2_ref, b2_ref, y_ref, ext_ref, cat_ref, x1_ref, xb_ref, acc_ref):
    tm = u_ref.shape[1]
    i = pl.program_id(1)

    @pl.when(i == 0)
    def _():
        ext_ref[0:POOL_HALO, :] = jnp.zeros((POOL_HALO, POOL_WIDTH), F32)

    u = u_ref[0]
    ext_ref[POOL_HALO:, :] = u
    t = (i * tm + lax.broadcasted_iota(jnp.int32, (tm, 1), 0) + 1).astype(F32)
    cat_ref[:, 0:NSA_WIDTH] = o_ref[0]
    for gi, w in enumerate(POOL_WINDOWS):
        cols = slice(gi * POOL_GROUP_DIM, (gi + 1) * POOL_GROUP_DIM)
        u_g = u[:, cols]
        acc = u_g
        for d in range(1, w):
            acc = acc + ext_ref[POOL_HALO - d:POOL_HALO - d + tm, cols]
        r = acc / jnp.minimum(t, float(w)) - u_g
        y_g = _dot(r.astype(BF16), pw_ref[gi]) * ps_ref[:, cols]
        cat_ref[:, NSA_WIDTH + gi * POOL_GROUP_DIM:NSA_WIDTH + (gi + 1) * POOL_GROUP_DIM] = y_g.astype(BF16)
    ext_ref[0:POOL_HALO, :] = u[tm - POOL_HALO:, :]
    mix = _dot(cat_ref[...], wo_ref[...])
    y_ref[0] = _norm_mlp_norm(x_ref[0], mix, g1_ref, b1_ref, w1_ref, w2_ref, g2_ref, b2_ref,
                              x1_ref, xb_ref, acc_ref)


def _post_even(o, u, x, wo, pw, ps, g1, b1, w1, w2, g2, b2):
    B, S, D = x.shape
    tm = PROJ_TM
    row = lambda n: pl.BlockSpec((1, tm, n), lambda bb, i: (bb, i, 0))
    full = lambda a: pl.BlockSpec(a.shape, lambda bb, i: (0,) * a.ndim)
    return pl.pallas_call(
        _post_even_kernel,
        grid=(B, S // tm),
        in_specs=[row(NSA_WIDTH), row(POOL_WIDTH), row(D), full(wo), full(pw), full(ps), full(g1), full(b1),
                  full(w1), full(w2), full(g2), full(b2)],
        out_specs=row(D),
        out_shape=jax.ShapeDtypeStruct((B, S, D), F32),
        scratch_shapes=[pltpu.VMEM((POOL_HALO + tm, POOL_WIDTH), F32),
                        pltpu.VMEM((tm, NSA_WIDTH + POOL_WIDTH), BF16)] + _post_scratch(tm),
        compiler_params=_params("parallel", "arbitrary"),
        name="post_even",
    )(o, u, x, wo, pw, ps, g1, b1, w1, w2, g2, b2)


def _proj_odd_kernel(x_ref, w_ref, wa_ref, gw_ref, gb_ref, q_ref, k_ref, v_ref, r_ref, la_ref):
    xb = x_ref[0].astype(BF16)
    n = FF_CHUNK
    base = 0
    for ref, width in ((q_ref, GLA_KEY_WIDTH), (k_ref, GLA_KEY_WIDTH), (v_ref, GLA_VAL_WIDTH),
                       (r_ref, GLA_VAL_WIDTH)):
        for c in range(width // n):
            ref[0, :, c * n:(c + 1) * n] = _dot(xb, w_ref[:, base + c * n:base + (c + 1) * n]).astype(ref.dtype)
        base += width
    a = _dot(xb, wa_ref[...])
    z = _dot(a.astype(BF16), gw_ref[...]) + gb_ref[...]
    log_sig = jnp.minimum(z, 0.0) - jnp.log(1.0 + jnp.exp(-jnp.abs(z)))
    la_ref[0] = log_sig / GLA_TAU


def _proj_odd(x, w, wa, gw, gb):
    B, S, D = x.shape
    tm = PROJ_TM
    row = lambda n: pl.BlockSpec((1, tm, n), lambda b, i: (b, i, 0))
    full = lambda a: pl.BlockSpec(a.shape, lambda b, i: (0,) * a.ndim)
    return pl.pallas_call(
        _proj_odd_kernel,
        grid=(B, S // tm),
        in_specs=[row(D), full(w), full(wa), full(gw), full(gb)],
        out_specs=(row(GLA_KEY_WIDTH), row(GLA_KEY_WIDTH), row(GLA_VAL_WIDTH), row(GLA_VAL_WIDTH),
                   row(GLA_KEY_WIDTH)),
        out_shape=(jax.ShapeDtypeStruct((B, S, GLA_KEY_WIDTH), F32),
                   jax.ShapeDtypeStruct((B, S, GLA_KEY_WIDTH), F32),
                   jax.ShapeDtypeStruct((B, S, GLA_VAL_WIDTH), BF16),
                   jax.ShapeDtypeStruct((B, S, GLA_VAL_WIDTH), F32),
                   jax.ShapeDtypeStruct((B, S, GLA_KEY_WIDTH), F32)),
        compiler_params=_params("parallel", "parallel"),
        name="proj_odd",
    )(x, w, wa, gw, gb)


def _gla_kernel(q_ref, k_ref, la_ref, v_ref, r_ref, ng_ref, o_ref, state_ref):
    C = GLA_CHUNK

    @pl.when(pl.program_id(1) == 0)
    def _():
        state_ref[...] = jnp.zeros(state_ref.shape, F32)

    row = lax.broadcasted_iota(jnp.int32, (C, GLA_DK), 0)
    causal = (lax.broadcasted_iota(jnp.int32, (C, C), 0) >= lax.broadcasted_iota(jnp.int32, (C, C), 1))
    for n in range(q_ref.shape[1] // C):
        rows = slice(n * C, (n + 1) * C)
        for h in range(GLA_HEADS):
            kcols = slice(h * GLA_DK, (h + 1) * GLA_DK)
            vcols = slice(h * GLA_DV, (h + 1) * GLA_DV)
            b = la_ref[0, rows, kcols]
            shift = 1
            while shift < C:
                b = b + jnp.where(row >= shift, pltpu.roll(b, shift, axis=0), 0.0)
                shift *= 2
            b_last = b[C - 1:C, :]
            q_t = (q_ref[0, rows, kcols] * (GLA_DK ** -0.5)) * jnp.exp(b)
            k = k_ref[0, rows, kcols]
            k_t = k * jnp.exp(-b)
            k_u = k * jnp.exp(b_last - b)
            v = v_ref[0, rows, vcols]
            q_tb = q_t.astype(BF16)
            att = jnp.where(causal, _dot_nt(q_tb, k_t.astype(BF16)), 0.0)
            state = state_ref[h]
            o = _dot(att.astype(BF16), v) + _dot_nt(q_tb, state.astype(BF16))
            upd = _dot_tn(v, k_u.astype(BF16))
            state_ref[h] = jnp.exp(b_last) * state + upd
            o = o * lax.rsqrt(jnp.mean(o * o, axis=-1, keepdims=True) + LN_EPS) * ng_ref[...]
            o_ref[0, rows, vcols] = (o * jax.nn.silu(r_ref[0, rows, vcols])).astype(BF16)


def _gla(q, k, la, v, r, ng):
    B, S, _ = q.shape
    row = lambda n: pl.BlockSpec((1, GLA_TILE, n), lambda b, i: (b, i, 0))
    return pl.pallas_call(
        _gla_kernel,
        grid=(B, S // GLA_TILE),
        in_specs=[row(GLA_KEY_WIDTH), row(GLA_KEY_WIDTH), row(GLA_KEY_WIDTH), row(GLA_VAL_WIDTH),
                  row(GLA_VAL_WIDTH), pl.BlockSpec(ng.shape, lambda b, i: (0, 0))],
        out_specs=row(GLA_VAL_WIDTH),
        out_shape=jax.ShapeDtypeStruct((B, S, GLA_VAL_WIDTH), BF16),
        scratch_shapes=[pltpu.VMEM((GLA_HEADS, GLA_DV, GLA_DK), F32)],
        compiler_params=_params("parallel", "arbitrary"),
        name="gla",
    )(q, k, la, v, r, ng)


def _post_odd_kernel(o_ref, x_ref, wo_ref, g1_ref, b1_ref, w1_ref, w2_ref, g2_ref, b2_ref, y_ref,
                     x1_ref, xb_ref, acc_ref):
    mix = _dot(o_ref[0], wo_ref[...])
    y_ref[0] = _norm_mlp_norm(x_ref[0], mix, g1_ref, b1_ref, w1_ref, w2_ref, g2_ref, b2_ref,
                              x1_ref, xb_ref, acc_ref)


def _post_odd(o, x, wo, g1, b1, w1, w2, g2, b2):
    B, S, D = x.shape
    tm = PROJ_TM
    row = lambda n: pl.BlockSpec((1, tm, n), lambda bb, i: (bb, i, 0))
    full = lambda a: pl.BlockSpec(a.shape, lambda bb, i: (0,) * a.ndim)
    return pl.pallas_call(
        _post_odd_kernel,
        grid=(B, S // tm),
        in_specs=[row(GLA_VAL_WIDTH), row(D), full(wo), full(g1), full(b1), full(w1), full(w2), full(g2),
                  full(b2)],
        out_specs=row(D),
        out_shape=jax.ShapeDtypeStruct((B, S, D), F32),
        scratch_shapes=_post_scratch(tm),
        compiler_params=_params("parallel", "parallel"),
        name="post_odd",
    )(o, x, wo, g1, b1, w1, w2, g2, b2)


def _even_weights(w_in, cmp_pos, cmp_w1, cmp_b1, cmp_w2, pool_w, pool_scale, w_out):
    o1 = NSA_WIDTH
    o2 = o1 + 6 * NSA_KV_WIDTH
    o3 = o2 + 3 * NSA_HEADS
    D = w_in.shape[0]
    wq = (w_in[:, :o1] * (NSA_DH ** -0.5 * LOG2_E)).reshape(D, NSA_GROUPS, NSA_HPG, NSA_DH)
    slots = jnp.zeros((D, NSA_GROUPS, NSA_HPG, NSA_GROUPS, NSA_DH), F32)
    for g in range(NSA_GROUPS):
        slots = slots.at[:, g, :, g, :].set(wq[:, g])
    wq = slots.reshape(D, NSA_HEADS * LANES)
    kv = [w_in[:, o1 + n * NSA_KV_WIDTH:o1 + (n + 1) * NSA_KV_WIDTH] for n in range(6)]
    k_cmp, v_cmp, k_slc, v_slc, k_win, v_win = kv
    wg = jnp.pad(w_in[:, o2:o3], ((0, 0), (0, GATE_ROWS - 3 * NSA_HEADS)))
    wt = jnp.concatenate([wq, v_slc, v_win, wg], axis=1).T.astype(BF16)
    wr = jnp.concatenate([k_cmp, v_cmp, k_slc, k_win, w_in[:, o3:]], axis=1).astype(BF16)
    half = CMP_BLOCK // 2
    eye = jnp.eye(NSA_GROUPS, dtype=F32)
    w1 = cmp_w1.reshape(2, 2, half, NSA_DH, NSA_DH)
    w1 = jnp.einsum('whldo,gk->whlgdko', w1, eye).reshape(2, 2, half * LANES, LANES).astype(BF16)
    pos = cmp_pos.reshape(2, 2, half, 1, NSA_DH)
    pos = jnp.broadcast_to(pos, (2, 2, half, NSA_GROUPS, NSA_DH)).reshape(2, 2, 1, half * LANES)
    b1 = jnp.tile(cmp_b1, (1, NSA_GROUPS)).reshape(2, 1, LANES)
    w2 = jnp.einsum('wdo,gk->wgdko', cmp_w2, eye).reshape(2, LANES, LANES).astype(BF16)
    return dict(wt=wt, wr=wr, pos=pos, w1=w1, b1=b1, w2=w2,
                pw=pool_w.astype(BF16), ps=pool_scale.reshape(1, POOL_WIDTH), wo=w_out.astype(BF16))


def _selection_constants(seq_len):
    n_cmp = (seq_len - CMP_BLOCK) // CMP_STRIDE + 1
    n_sel = seq_len // SEL_BLOCK
    sub = np.arange(n_cmp)[:, None] + np.arange(CMP_BLOCK // CMP_STRIDE)[None, :]
    owner = sub // (SEL_BLOCK // CMP_STRIDE)
    sel_map = (owner[:, :, None] == np.arange(n_sel)[None, None, :]).sum(1).astype(np.float32)
    selt = np.zeros((n_sel, seq_len // CMP_STRIDE), np.float32)
    selt[:, :n_cmp] = sel_map.T
    eb = np.zeros((seq_len, LANES), np.float32)
    eb[np.arange(seq_len), np.arange(seq_len) // SEL_BLOCK] = MASK_VALUE
    return jnp.asarray(selt, BF16), jnp.asarray(eb, BF16)


def _even_layer(x, w, mlp, selt, eb):
    B, S, _ = x.shape
    qt, vst, vwt, gt, kc, vc, ks, kw, u = _proj_even(x, w['wt'], w['wr'], eb)
    kcc, vcct = _compress(kc, vc, w['pos'], w['w1'], w['b1'], w['w2'])
    o = _nsa_attention(qt, kcc, vcct, ks, vst, kw, vwt, gt, selt)
    return _post_even(o, u, x, w['wo'], w['pw'], w['ps'], *mlp)


def _odd_layer(x, w_in, gate_w2, gate_b, norm_g, w_out, mlp):
    wb = w_in.astype(BF16)
    wa = jnp.pad(wb[:, 2 * GLA_KEY_WIDTH + 2 * GLA_VAL_WIDTH:], ((0, 0), (0, LANES - GLA_RANK)))
    gw = jnp.pad(gate_w2.astype(BF16), ((0, LANES - GLA_RANK), (0, 0)))
    q, k, v, r, la = _proj_odd(x, wb, wa, gw, gate_b.reshape(1, GLA_KEY_WIDTH))
    o = _gla(q, k, la, v, r, norm_g.reshape(1, GLA_DV))
    return _post_odd(o, x, w_out.astype(BF16), *mlp)


def kernel(x, a_w_in, a_cmp_pos, a_cmp_w1, a_cmp_b1, a_cmp_w2, a_pool_w, a_pool_scale, a_w_out, c_w_in, c_gate_w2, c_gate_b, c_norm_g, c_w_out, ln1_g, ln1_b, ln2_g, ln2_b, mlp_w1, mlp_w2):
    S = x.shape[1]
    selt, eb = _selection_constants(S)
    for i in range(DEPTH):
        j = i // 2
        mlp = (ln1_g[i].reshape(1, D_MODEL), ln1_b[i].reshape(1, D_MODEL),
               mlp_w1[i].astype(BF16), mlp_w2[i].astype(BF16),
               ln2_g[i].reshape(1, D_MODEL), ln2_b[i].reshape(1, D_MODEL))
        if i % 2 == 0:
            w = _even_weights(a_w_in[j], a_cmp_pos[j], a_cmp_w1[j], a_cmp_b1[j], a_cmp_w2[j],
                              a_pool_w[j], a_pool_scale[j], a_w_out[j])
            x = _even_layer(x, w, mlp, selt, eb)
        else:
            x = _odd_layer(x, c_w_in[j], c_gate_w2[j], c_gate_b[j], c_norm_g[j], c_w_out[j], mlp)
    return x
```

```python
import functools

import numpy as np
import jax
import jax.numpy as jnp
from jax import lax
from jax.experimental import pallas as pl
from jax.experimental.pallas import tpu as pltpu

F32 = jnp.float32
BF16 = jnp.bfloat16

D_MODEL = 1024
DEPTH = 4
NSA_HEADS = 8
NSA_GROUPS = 2
NSA_HPG = NSA_HEADS // NSA_GROUPS
NSA_DH = 64
NSA_WIDTH = NSA_HEADS * NSA_DH
NSA_KV_WIDTH = NSA_GROUPS * NSA_DH
CMP_BLOCK = 32
CMP_STRIDE = 16
SEL_BLOCK = 64
SEL_TOP_N = 16
WINDOW = 512
POOL_WINDOWS = (2, 4, 8, 16)
POOL_GROUP_DIM = 128
POOL_WIDTH = 512
GLA_HEADS = 4
GLA_DK = 128
GLA_DV = 256
GLA_KEY_WIDTH = GLA_HEADS * GLA_DK
GLA_VAL_WIDTH = GLA_HEADS * GLA_DV
GLA_RANK = 16
GLA_TAU = 16.0
GLA_CHUNK = 64
D_FF = 4 * D_MODEL
ALPHA = (2 * DEPTH) ** 0.25
LN_EPS = 1e-5
MASK_VALUE = -1e30
FORCE_SCORE = 1e4
LOG2_E = 1.4426950408889634

LANES = 128
VMEM_LIMIT_BYTES = 56 * 1024 * 1024

PROJ_TM = 512
POST_TM = 1024
POST_SUB = 512
ATT_TQ = 256
ATT_CK = 512
FF_CHUNK = 512
POOL_HALO = 16
GLA_TILE = 512
GATE_ROWS = 32
PV_ROWS = 80
PROJ_T_ROWS = NSA_HEADS * LANES + 2 * LANES + GATE_ROWS


def _dot(a, b):
    return jnp.dot(a, b, preferred_element_type=F32)


def _dot_nt(a, b):
    return lax.dot_general(a, b, (((1,), (1,)), ((), ())), preferred_element_type=F32)


def _dot_tn(a, b):
    return lax.dot_general(a, b, (((0,), (0,)), ((), ())), preferred_element_type=F32)


def _params(*semantics):
    return pltpu.CompilerParams(dimension_semantics=semantics, vmem_limit_bytes=VMEM_LIMIT_BYTES)


def _exp2_bf16(d):
    return jnp.exp2(d.astype(BF16))


def _layer_norm(z, g, b):
    mu = jnp.mean(z, axis=-1, keepdims=True)
    zc = z - mu
    var = jnp.mean(zc * zc, axis=-1, keepdims=True)
    return zc * lax.rsqrt(var + LN_EPS) * g + b


def _proj_even_kernel(x_ref, wt_ref, wr_ref, eb_ref,
                      qt_ref, vst_ref, vwt_ref, gt_ref, kc_ref, vc_ref, ks_ref, kw_ref, u_ref, cmp_ref):
    xb = x_ref[0].astype(BF16)
    ht = _dot_nt(wt_ref[...], xb)
    nq = NSA_HEADS * LANES
    for h in range(NSA_HEADS):
        qt_ref[0, h] = ht[h * LANES:(h + 1) * LANES].astype(BF16)
    pad_rows = lax.broadcasted_iota(jnp.int32, (PV_ROWS - NSA_DH, ht.shape[1]), 0)
    ones_then_zeros = jnp.where(pad_rows == 0, 1.0, 0.0)
    for g in range(NSA_GROUPS):
        for ref, base in ((vst_ref, nq), (vwt_ref, nq + LANES)):
            v_g = ht[base + g * NSA_DH:base + (g + 1) * NSA_DH]
            ref[0, g] = jnp.concatenate([v_g, ones_then_zeros], axis=0).astype(BF16)
    gt_ref[0] = jax.nn.sigmoid(ht[nq + 2 * LANES:])
    hr = _dot(xb, wr_ref[...])
    n_rows = hr.shape[0] // CMP_STRIDE
    for which, dst in enumerate((kc_ref, vc_ref)):
        cmp_ref[which] = hr[:, which * LANES:(which + 1) * LANES]
        for l in range(CMP_STRIDE):
            dst[0, :, l * LANES:(l + 1) * LANES] = cmp_ref[which, pl.ds(l, n_rows, stride=CMP_STRIDE), :]
    ks_ref[0, :, :LANES] = hr[:, 2 * LANES:3 * LANES].astype(BF16)
    ks_ref[0, :, LANES:] = eb_ref[...]
    kw_ref[0] = hr[:, 3 * LANES:4 * LANES].astype(BF16)
    u_ref[0] = hr[:, 4 * LANES:]


def _proj_even(x, wt, wr, eb):
    B, S, D = x.shape
    tm = PROJ_TM
    row = lambda n: pl.BlockSpec((1, tm, n), lambda b, i: (b, i, 0))
    col = lambda n: pl.BlockSpec((1, n, tm), lambda b, i: (b, 0, i))
    full = lambda a: pl.BlockSpec(a.shape, lambda b, i: (0,) * a.ndim)
    out_shape = (
        jax.ShapeDtypeStruct((B, NSA_HEADS, LANES, S), BF16),
        jax.ShapeDtypeStruct((B, NSA_GROUPS, PV_ROWS, S), BF16),
        jax.ShapeDtypeStruct((B, NSA_GROUPS, PV_ROWS, S), BF16),
        jax.ShapeDtypeStruct((B, GATE_ROWS, S), F32),
        jax.ShapeDtypeStruct((B, S // CMP_STRIDE, CMP_STRIDE * LANES), F32),
        jax.ShapeDtypeStruct((B, S // CMP_STRIDE, CMP_STRIDE * LANES), F32),
        jax.ShapeDtypeStruct((B, S, 2 * LANES), BF16),
        jax.ShapeDtypeStruct((B, S, LANES), BF16),
        jax.ShapeDtypeStruct((B, S, POOL_WIDTH), F32),
    )
    out_specs = (
        pl.BlockSpec((1, NSA_HEADS, LANES, tm), lambda b, i: (b, 0, 0, i)),
        pl.BlockSpec((1, NSA_GROUPS, PV_ROWS, tm), lambda b, i: (b, 0, 0, i)),
        pl.BlockSpec((1, NSA_GROUPS, PV_ROWS, tm), lambda b, i: (b, 0, 0, i)), col(GATE_ROWS),
        pl.BlockSpec((1, tm // CMP_STRIDE, CMP_STRIDE * LANES), lambda b, i: (b, i, 0)),
        pl.BlockSpec((1, tm // CMP_STRIDE, CMP_STRIDE * LANES), lambda b, i: (b, i, 0)),
        row(2 * LANES), row(LANES), row(POOL_WIDTH),
    )
    return pl.pallas_call(
        _proj_even_kernel,
        grid=(B, S // tm),
        in_specs=[row(D), full(wt), full(wr), pl.BlockSpec((tm, LANES), lambda b, i: (i, 0))],
        out_specs=out_specs,
        out_shape=out_shape,
        scratch_shapes=[pltpu.VMEM((2, tm, LANES), F32)],
        compiler_params=_params("parallel", "parallel"),
        name="proj_even",
    )(x, wt, wr, eb)


def _gelu_tanh(x):
    return 0.5 * x * (1.0 + jnp.tanh(0.7978845608028654 * (x + 0.044715 * (x * x * x))))


def _compress_kernel(kc_ref, vc_ref, pos_ref, w1_ref, b1_ref, w2_ref, okc_ref, ovct_ref):
    for which, src in enumerate((kc_ref, vc_ref)):
        xr = src[0]
        nr = xr.shape[0]
        lo = _dot((xr + pos_ref[which, 0]).astype(BF16), w1_ref[which, 0])
        hi = _dot((xr + pos_ref[which, 1]).astype(BF16), w1_ref[which, 1])
        hi_next = pltpu.roll(hi, nr - 1, axis=0)
        act = _gelu_tanh(lo + hi_next + b1_ref[which])
        out = _dot(act.astype(BF16), w2_ref[which])
        rows = lax.broadcasted_iota(jnp.int32, out.shape, 0)
        out = jnp.where(rows < nr - 1, out, 0.0)
        if which == 0:
            okc_ref[0] = out.astype(BF16)
        else:
            ovct_ref[0] = out.T.astype(BF16)


def _compress(kc, vc, pos, w1, b1, w2):
    B, NR, W = kc.shape
    blk = pl.BlockSpec((1, NR, W), lambda b: (b, 0, 0))
    full = lambda a: pl.BlockSpec(a.shape, lambda b: (0,) * a.ndim)
    return pl.pallas_call(
        _compress_kernel,
        grid=(B,),
        in_specs=[blk, blk, full(pos), full(w1), full(b1), full(w2)],
        out_specs=(pl.BlockSpec((1, NR, LANES), lambda b: (b, 0, 0)),
                   pl.BlockSpec((1, LANES, NR), lambda b: (b, 0, 0))),
        out_shape=(jax.ShapeDtypeStruct((B, NR, LANES), BF16),
                   jax.ShapeDtypeStruct((B, LANES, NR), BF16)),
        compiler_params=_params("parallel"),
        name="compress",
    )(kc, vc, pos, w1, b1, w2)


def _nsa_kernel(qt_ref, kc_ref, vct_ref, ks_ref, vst_ref, kw_ref, vwt_ref, gt_ref, selt_ref, o_ref,
                qsel_ref, oc_ref, s_ref, acc_ref, *, seq_len):
    tq = ATT_TQ
    cols = NSA_HPG * tq
    n_sel = seq_len // SEL_BLOCK
    t0 = pl.program_id(1) * tq
    t_q = t0 + lax.broadcasted_iota(jnp.int32, (1, tq), 1)
    t_cols = jnp.concatenate([t_q] * NSA_HPG, axis=1)

    for g in range(NSA_GROUPS):
        qt = jnp.concatenate([qt_ref[0, NSA_HPG * g + h] for h in range(NSA_HPG)], axis=1)

        s_c = _dot(kc_ref[0], qt)
        c_end = lax.broadcasted_iota(jnp.int32, s_c.shape, 0) * CMP_STRIDE + (CMP_BLOCK - 1)
        valid_c = c_end <= t_cols
        s_c = jnp.where(valid_c, s_c, MASK_VALUE)
        e_c = jnp.where(valid_c, jnp.exp2(s_c - jnp.max(s_c, axis=0, keepdims=True)), 0.0)
        p_c = e_c * (1.0 / jnp.maximum(jnp.sum(e_c, axis=0, keepdims=True), 1e-30))
        oc_ref[g] = _dot(vct_ref[0], p_c.astype(BF16))[g * NSA_DH:(g + 1) * NSA_DH]

        p_sum = p_c[:, 0:tq]
        for h in range(1, NSA_HPG):
            p_sum = p_sum + p_c[:, h * tq:(h + 1) * tq]
        p_hi = p_sum.astype(BF16)
        p_lo = (p_sum - p_hi.astype(F32)).astype(BF16)
        imp = _dot(selt_ref[...], p_hi) + _dot(selt_ref[...], p_lo)
        blk = lax.broadcasted_iota(jnp.int32, imp.shape, 0)
        cur = t_q >> 6
        forced = (blk == 0) | (blk == cur) | (blk == cur - 1)
        score = jnp.where(forced, FORCE_SCORE, jnp.where(blk > cur, -FORCE_SCORE, imp))
        beaten = jnp.zeros(imp.shape, F32)
        for i in range(n_sel):
            s_i = score[i:i + 1, :]
            wins = (s_i > score) | ((s_i == score) & (blk > i))
            beaten = beaten + jnp.where(wins, 1.0, 0.0)
        not_sel = jnp.where(beaten < float(SEL_TOP_N), 0.0, 1.0)
        not_sel = jnp.concatenate([not_sel, jnp.zeros((LANES - n_sel, tq), F32)], axis=0).astype(BF16)
        qsel_ref[g, 0:LANES, :] = qt
        qsel_ref[g, LANES:, :] = jnp.concatenate([not_sel] * NSA_HPG, axis=1)

    def score(c, maxes, on_diagonal):
        k0 = pl.multiple_of(c * ATT_CK, ATT_CK)
        k_blk = ks_ref[0, pl.ds(k0, ATT_CK), :]
        out = []
        for g in range(NSA_GROUPS):
            s = _dot(k_blk, qsel_ref[g])
            if on_diagonal:
                kpos = k0 + lax.broadcasted_iota(jnp.int32, s.shape, 0)
                s = jnp.where(kpos <= t_cols, s, MASK_VALUE)
            s_ref[g, pl.ds(k0, ATT_CK), :] = s
            out.append(jnp.maximum(maxes[g], jnp.max(s, axis=0, keepdims=True)))
        return tuple(out)

    last = (t0 + tq - 1) // ATT_CK
    m_init = jnp.full((1, cols), MASK_VALUE, F32)
    maxes = lax.fori_loop(0, last, lambda c, m: score(c, m, False), (m_init,) * NSA_GROUPS)
    maxes = score(last, maxes, True)

    acc_ref[...] = jnp.zeros(acc_ref.shape, F32)

    def attend(c, carry):
        k0 = pl.multiple_of(c * ATT_CK, ATT_CK)
        for g in range(NSA_GROUPS):
            p = _exp2_bf16(s_ref[g, pl.ds(k0, ATT_CK), :] - maxes[g])
            acc_ref[g] += _dot(vst_ref[0, g, :, pl.ds(k0, ATT_CK)], p)
        return carry

    lax.fori_loop(0, last + 1, attend, 0)

    w0 = pl.multiple_of(jnp.maximum(t0 - WINDOW, 0), tq)
    wlen = WINDOW + tq
    wpos = w0 + lax.broadcasted_iota(jnp.int32, (wlen, cols), 0)
    in_window = (wpos <= t_cols) & (wpos > t_cols - WINDOW)
    k_win = kw_ref[0, pl.ds(w0, wlen), :]
    s_win = [jnp.where(in_window, _dot(k_win, qsel_ref[g, 0:LANES, :]), MASK_VALUE)
             for g in range(NSA_GROUPS)]
    acc_win = [_dot(vwt_ref[0, g, :, pl.ds(w0, wlen)],
                    _exp2_bf16(s_win[g] - jnp.max(s_win[g], axis=0, keepdims=True)))
               for g in range(NSA_GROUPS)]

    gates = gt_ref[0]
    for g in range(NSA_GROUPS):
        acc_s = acc_ref[g]
        o_s = acc_s[0:NSA_DH] * (1.0 / jnp.maximum(acc_s[NSA_DH:NSA_DH + 1], 1e-30))
        o_w = acc_win[g][0:NSA_DH] * (1.0 / jnp.maximum(acc_win[g][NSA_DH:NSA_DH + 1], 1e-30))

        o_c = oc_ref[g]
        heads = []
        for h in range(NSA_HPG):
            c = slice(h * tq, (h + 1) * tq)
            gi = (NSA_HPG * g + h) * 3
            mix = (gates[gi:gi + 1] * o_c[:, c] + gates[gi + 1:gi + 2] * o_s[:, c]
                   + gates[gi + 2:gi + 3] * o_w[:, c])
            heads.append(mix)
        for j in range(NSA_HPG // 2):
            pair = jnp.concatenate([heads[2 * j], heads[2 * j + 1]], axis=0)
            c0 = (NSA_HPG * g + 2 * j) * NSA_DH
            o_ref[0, :, c0:c0 + LANES] = pair.T.astype(BF16)


def _nsa_attention(qt, kc, vct, ks, vst, kw, vwt, gt, selt):
    B, _, _, S = qt.shape
    tq = ATT_TQ
    cols = NSA_HPG * tq
    seq_rows = lambda n: pl.BlockSpec((1, S, n), lambda b, i: (b, 0, 0))
    seq_cols = pl.BlockSpec((1, NSA_GROUPS, PV_ROWS, S), lambda b, i: (b, 0, 0, 0))
    whole = lambda a: pl.BlockSpec((1,) + a.shape[1:], lambda b, i: (b, 0, 0))
    return pl.pallas_call(
        functools.partial(_nsa_kernel, seq_len=S),
        grid=(B, S // tq),
        in_specs=[pl.BlockSpec((1, NSA_HEADS, LANES, tq), lambda b, i: (b, 0, 0, i)),
                  whole(kc), whole(vct), seq_rows(2 * LANES), seq_cols, seq_rows(LANES),
                  seq_cols, pl.BlockSpec((1, GATE_ROWS, tq), lambda b, i: (b, 0, i)),
                  pl.BlockSpec(selt.shape, lambda b, i: (0, 0))],
        out_specs=pl.BlockSpec((1, tq, NSA_WIDTH), lambda b, i: (b, i, 0)),
        out_shape=jax.ShapeDtypeStruct((B, S, NSA_WIDTH), BF16),
        scratch_shapes=[pltpu.VMEM((NSA_GROUPS, 2 * LANES, cols), BF16),
                        pltpu.VMEM((NSA_GROUPS, NSA_DH, cols), F32),
                        pltpu.VMEM((NSA_GROUPS, S, cols), F32),
                        pltpu.VMEM((NSA_GROUPS, PV_ROWS, cols), F32)],
        compiler_params=_params("parallel", "parallel"),
        name="nsa_attention",
    )(qt, kc, vct, ks, vst, kw, vwt, gt, selt)


def _norm_mlp_norm(mix_fns, x_ref, y_ref, g1_ref, b1_ref, w1_ref, w2_ref, g2_ref, b2_ref, x1_ref, xb_ref, acc_ref):
    tiles = [slice(i * POST_SUB, (i + 1) * POST_SUB) for i in range(len(mix_fns))]
    n_ff = D_FF // FF_CHUNK

    def norm1(i):
        rows = tiles[i]
        x1_ref[rows] = _layer_norm(ALPHA * x_ref[0, rows] + mix_fns[i](), g1_ref[...], b1_ref[...])
        xb_ref[rows] = x1_ref[rows].astype(BF16)

    def mlp(i, chunks):
        rows = tiles[i]
        for c in chunks:
            cols = slice(c * FF_CHUNK, (c + 1) * FF_CHUNK)
            h = jnp.maximum(_dot(xb_ref[rows], w1_ref[:, cols]), 0.0)
            part = _dot((h * h).astype(BF16), w2_ref[cols, :])
            if c == 0:
                acc_ref[rows] = part
            else:
                acc_ref[rows] += part

    def norm2(i):
        rows = tiles[i]
        y_ref[0, rows] = _layer_norm(ALPHA * x1_ref[rows] + acc_ref[rows], g2_ref[...], b2_ref[...])

    first, second = range(0, n_ff // 2), range(n_ff // 2, n_ff)
    norm1(0)
    for i in range(len(tiles)):
        mlp(i, first)
        if i > 0:
            norm2(i - 1)
        if i + 1 < len(tiles):
            norm1(i + 1)
        mlp(i, second)
    norm2(len(tiles) - 1)


def _post_scratch(tm):
    return [pltpu.VMEM((tm, D_MODEL), F32), pltpu.VMEM((tm, D_MODEL), BF16), pltpu.VMEM((tm, D_MODEL), F32)]


def _post_even_kernel(o_ref, u_ref, x_ref, wo_ref, pw_ref, ps_ref, g1_ref, b1_ref, w1_ref, w2_ref,
                      g2_ref, b2_ref, y_ref, ext_ref, cat_ref, x1_ref, xb_ref, acc_ref):
    tm = u_ref.shape[1]
    i = pl.program_id(1)

    @pl.when(i == 0)
    def _():
        ext_ref[0:POOL_HALO, :] = jnp.zeros((POOL_HALO, POOL_WIDTH), F32)

    ext_ref[POOL_HALO:, :] = u_ref[0]

    def mix(j):
        r0 = j * POST_SUB
        rows = slice(r0, r0 + POST_SUB)
        t = (i * tm + r0 + lax.broadcasted_iota(jnp.int32, (POST_SUB, 1), 0) + 1).astype(F32)
        cat_ref[rows, 0:NSA_WIDTH] = o_ref[0, rows]
        for gi, w in enumerate(POOL_WINDOWS):
            cols = slice(gi * POOL_GROUP_DIM, (gi + 1) * POOL_GROUP_DIM)
            u_g = u_ref[0, rows, cols]
            acc = u_g
            for d in range(1, w):
                acc = acc + ext_ref[POOL_HALO + r0 - d:POOL_HALO + r0 - d + POST_SUB, cols]
            r = acc / jnp.minimum(t, float(w)) - u_g
            y_g = _dot(r.astype(BF16), pw_ref[gi]) * ps_ref[:, cols]
            cat_ref[rows, NSA_WIDTH + gi * POOL_GROUP_DIM:NSA_WIDTH + (gi + 1) * POOL_GROUP_DIM] = y_g.astype(BF16)
        return _dot(cat_ref[rows], wo_ref[...])

    mix_fns = [functools.partial(mix, j) for j in range(tm // POST_SUB)]
    _norm_mlp_norm(mix_fns, x_ref, y_ref, g1_ref, b1_ref, w1_ref, w2_ref, g2_ref, b2_ref,
                   x1_ref, xb_ref, acc_ref)
    ext_ref[0:POOL_HALO, :] = u_ref[0, tm - POOL_HALO:, :]


def _layer_weight_spec(stacked, layer):
    return pl.BlockSpec((None,) + stacked.shape[1:], lambda bb, i: (layer,) + (0,) * (stacked.ndim - 1),
                        pipeline_mode=pl.Buffered(1))


def _post_even(o, u, x, wo, pw, ps, g1, b1, w1, w2, g2, b2, layer):
    B, S, D = x.shape
    tm = POST_TM
    row = lambda n: pl.BlockSpec((1, tm, n), lambda bb, i: (bb, i, 0))
    full = lambda a: pl.BlockSpec(a.shape, lambda bb, i: (0,) * a.ndim)
    return pl.pallas_call(
        _post_even_kernel,
        grid=(B, S // tm),
        in_specs=[row(NSA_WIDTH), row(POOL_WIDTH), row(D), full(wo), full(pw), full(ps), full(g1), full(b1),
                  _layer_weight_spec(w1, layer), _layer_weight_spec(w2, layer), full(g2), full(b2)],
        out_specs=row(D),
        out_shape=jax.ShapeDtypeStruct((B, S, D), F32),
        scratch_shapes=[pltpu.VMEM((POOL_HALO + tm, POOL_WIDTH), F32),
                        pltpu.VMEM((tm, NSA_WIDTH + POOL_WIDTH), BF16)] + _post_scratch(tm),
        compiler_params=_params("parallel", "arbitrary"),
        name="post_even",
    )(o, u, x, wo, pw, ps, g1, b1, w1, w2, g2, b2)


def _proj_odd_kernel(x_ref, w_ref, wa_ref, gw_ref, gb_ref, q_ref, k_ref, v_ref, r_ref, la_ref):
    xb = x_ref[0].astype(BF16)
    n = FF_CHUNK
    base = 0
    for ref, width in ((q_ref, GLA_KEY_WIDTH), (k_ref, GLA_KEY_WIDTH), (v_ref, GLA_VAL_WIDTH),
                       (r_ref, GLA_VAL_WIDTH)):
        for c in range(width // n):
            ref[0, :, c * n:(c + 1) * n] = _dot(xb, w_ref[:, base + c * n:base + (c + 1) * n]).astype(ref.dtype)
        base += width
    a = _dot(xb, wa_ref[...])
    z = _dot(a.astype(BF16), gw_ref[...]) + gb_ref[...]
    log_sig = jnp.minimum(z, 0.0) - jnp.log(1.0 + jnp.exp(-jnp.abs(z)))
    la_ref[0] = log_sig / GLA_TAU


def _proj_odd(x, w, wa, gw, gb):
    B, S, D = x.shape
    tm = PROJ_TM
    row = lambda n: pl.BlockSpec((1, tm, n), lambda b, i: (b, i, 0))
    full = lambda a: pl.BlockSpec(a.shape, lambda b, i: (0,) * a.ndim)
    return pl.pallas_call(
        _proj_odd_kernel,
        grid=(B, S // tm),
        in_specs=[row(D), full(w), full(wa), full(gw), full(gb)],
        out_specs=(row(GLA_KEY_WIDTH), row(GLA_KEY_WIDTH), row(GLA_VAL_WIDTH), row(GLA_VAL_WIDTH),
                   row(GLA_KEY_WIDTH)),
        out_shape=(jax.ShapeDtypeStruct((B, S, GLA_KEY_WIDTH), F32),
                   jax.ShapeDtypeStruct((B, S, GLA_KEY_WIDTH), F32),
                   jax.ShapeDtypeStruct((B, S, GLA_VAL_WIDTH), BF16),
                   jax.ShapeDtypeStruct((B, S, GLA_VAL_WIDTH), F32),
                   jax.ShapeDtypeStruct((B, S, GLA_KEY_WIDTH), F32)),
        compiler_params=_params("parallel", "parallel"),
        name="proj_odd",
    )(x, w, wa, gw, gb)


def _gla_kernel(q_ref, k_ref, la_ref, v_ref, r_ref, ng_ref, o_ref, state_ref):
    C = GLA_CHUNK

    @pl.when(pl.program_id(1) == 0)
    def _():
        state_ref[...] = jnp.zeros(state_ref.shape, F32)

    row = lax.broadcasted_iota(jnp.int32, (C, GLA_DK), 0)
    causal = (lax.broadcasted_iota(jnp.int32, (C, C), 0) >= lax.broadcasted_iota(jnp.int32, (C, C), 1))
    for n in range(q_ref.shape[1] // C):
        rows = slice(n * C, (n + 1) * C)
        for h in range(GLA_HEADS):
            kcols = slice(h * GLA_DK, (h + 1) * GLA_DK)
            vcols = slice(h * GLA_DV, (h + 1) * GLA_DV)
            b = la_ref[0, rows, kcols]
            shift = 1
            while shift < C:
                b = b + jnp.where(row >= shift, pltpu.roll(b, shift, axis=0), 0.0)
                shift *= 2
            b_last = b[C - 1:C, :]
            q_t = (q_ref[0, rows, kcols] * (GLA_DK ** -0.5)) * jnp.exp(b)
            k = k_ref[0, rows, kcols]
            k_t = k * jnp.exp(-b)
            k_u = k * jnp.exp(b_last - b)
            v = v_ref[0, rows, vcols]
            q_tb = q_t.astype(BF16)
            att = jnp.where(causal, _dot_nt(q_tb, k_t.astype(BF16)), 0.0)
            state = state_ref[h]
            o = _dot(att.astype(BF16), v) + _dot_nt(q_tb, state.astype(BF16))
            upd = _dot_tn(v, k_u.astype(BF16))
            state_ref[h] = jnp.exp(b_last) * state + upd
            o = o * lax.rsqrt(jnp.mean(o * o, axis=-1, keepdims=True) + LN_EPS) * ng_ref[...]
            o_ref[0, rows, vcols] = (o * jax.nn.silu(r_ref[0, rows, vcols])).astype(BF16)


def _gla(q, k, la, v, r, ng):
    B, S, _ = q.shape
    row = lambda n: pl.BlockSpec((1, GLA_TILE, n), lambda b, i: (b, i, 0))
    return pl.pallas_call(
        _gla_kernel,
        grid=(B, S // GLA_TILE),
        in_specs=[row(GLA_KEY_WIDTH), row(GLA_KEY_WIDTH), row(GLA_KEY_WIDTH), row(GLA_VAL_WIDTH),
                  row(GLA_VAL_WIDTH), pl.BlockSpec(ng.shape, lambda b, i: (0, 0))],
        out_specs=row(GLA_VAL_WIDTH),
        out_shape=jax.ShapeDtypeStruct((B, S, GLA_VAL_WIDTH), BF16),
        scratch_shapes=[pltpu.VMEM((GLA_HEADS, GLA_DV, GLA_DK), F32)],
        compiler_params=_params("parallel", "arbitrary"),
        name="gla",
    )(q, k, la, v, r, ng)


def _post_odd_kernel(o_ref, x_ref, wo_ref, g1_ref, b1_ref, w1_ref, w2_ref, g2_ref, b2_ref, y_ref,
                     x1_ref, xb_ref, acc_ref):
    def mix(i):
        return _dot(o_ref[0, i * POST_SUB:(i + 1) * POST_SUB], wo_ref[...])

    mix_fns = [functools.partial(mix, i) for i in range(o_ref.shape[1] // POST_SUB)]
    _norm_mlp_norm(mix_fns, x_ref, y_ref, g1_ref, b1_ref, w1_ref, w2_ref, g2_ref, b2_ref,
                   x1_ref, xb_ref, acc_ref)


def _post_odd(o, x, wo, g1, b1, w1, w2, g2, b2, layer):
    B, S, D = x.shape
    tm = POST_TM
    row = lambda n: pl.BlockSpec((1, tm, n), lambda bb, i: (bb, i, 0))
    full = lambda a: pl.BlockSpec(a.shape, lambda bb, i: (0,) * a.ndim)
    return pl.pallas_call(
        _post_odd_kernel,
        grid=(B, S // tm),
        in_specs=[row(GLA_VAL_WIDTH), row(D), full(wo), full(g1), full(b1), _layer_weight_spec(w1, layer),
                  _layer_weight_spec(w2, layer), full(g2), full(b2)],
        out_specs=row(D),
        out_shape=jax.ShapeDtypeStruct((B, S, D), F32),
        scratch_shapes=_post_scratch(tm),
        compiler_params=_params("parallel", "parallel"),
        name="post_odd",
    )(o, x, wo, g1, b1, w1, w2, g2, b2)


def _even_weights(w_in, cmp_pos, cmp_w1, cmp_b1, cmp_w2, pool_w, pool_scale, w_out):
    o1 = NSA_WIDTH
    o2 = o1 + 6 * NSA_KV_WIDTH
    o3 = o2 + 3 * NSA_HEADS
    D = w_in.shape[0]
    wq = (w_in[:, :o1] * (NSA_DH ** -0.5 * LOG2_E)).reshape(D, NSA_GROUPS, NSA_HPG, NSA_DH)
    slots = jnp.zeros((D, NSA_GROUPS, NSA_HPG, NSA_GROUPS, NSA_DH), F32)
    for g in range(NSA_GROUPS):
        slots = slots.at[:, g, :, g, :].set(wq[:, g])
    wq = slots.reshape(D, NSA_HEADS * LANES)
    kv = [w_in[:, o1 + n * NSA_KV_WIDTH:o1 + (n + 1) * NSA_KV_WIDTH] for n in range(6)]
    k_cmp, v_cmp, k_slc, v_slc, k_win, v_win = kv
    wg = jnp.pad(w_in[:, o2:o3], ((0, 0), (0, GATE_ROWS - 3 * NSA_HEADS)))
    wt = jnp.concatenate([wq, v_slc, v_win, wg], axis=1).T.astype(BF16)
    wr = jnp.concatenate([k_cmp, v_cmp, k_slc, k_win, w_in[:, o3:]], axis=1).astype(BF16)
    half = CMP_BLOCK // 2
    eye = jnp.eye(NSA_GROUPS, dtype=F32)
    w1 = cmp_w1.reshape(2, 2, half, NSA_DH, NSA_DH)
    w1 = jnp.einsum('whldo,gk->whlgdko', w1, eye).reshape(2, 2, half * LANES, LANES).astype(BF16)
    pos = cmp_pos.reshape(2, 2, half, 1, NSA_DH)
    pos = jnp.broadcast_to(pos, (2, 2, half, NSA_GROUPS, NSA_DH)).reshape(2, 2, 1, half * LANES)
    b1 = jnp.tile(cmp_b1, (1, NSA_GROUPS)).reshape(2, 1, LANES)
    w2 = jnp.einsum('wdo,gk->wgdko', cmp_w2, eye).reshape(2, LANES, LANES).astype(BF16)
    return dict(wt=wt, wr=wr, pos=pos, w1=w1, b1=b1, w2=w2,
                pw=pool_w.astype(BF16), ps=pool_scale.reshape(1, POOL_WIDTH), wo=w_out.astype(BF16))


def _selection_constants(seq_len):
    n_cmp = (seq_len - CMP_BLOCK) // CMP_STRIDE + 1
    n_sel = seq_len // SEL_BLOCK
    sub = np.arange(n_cmp)[:, None] + np.arange(CMP_BLOCK // CMP_STRIDE)[None, :]
    owner = sub // (SEL_BLOCK // CMP_STRIDE)
    sel_map = (owner[:, :, None] == np.arange(n_sel)[None, None, :]).sum(1).astype(np.float32)
    selt = np.zeros((n_sel, seq_len // CMP_STRIDE), np.float32)
    selt[:, :n_cmp] = sel_map.T
    eb = np.zeros((seq_len, LANES), np.float32)
    eb[np.arange(seq_len), np.arange(seq_len) // SEL_BLOCK] = MASK_VALUE
    return jnp.asarray(selt, BF16), jnp.asarray(eb, BF16)


def _even_layer(x, w, mlp, selt, eb):
    B, S, _ = x.shape
    qt, vst, vwt, gt, kc, vc, ks, kw, u = _proj_even(x, w['wt'], w['wr'], eb)
    kcc, vcct = _compress(kc, vc, w['pos'], w['w1'], w['b1'], w['w2'])
    o = _nsa_attention(qt, kcc, vcct, ks, vst, kw, vwt, gt, selt)
    return _post_even(o, u, x, w['wo'], w['pw'], w['ps'], *mlp)


def _odd_layer(x, w_in, gate_w2, gate_b, norm_g, w_out, mlp):
    wb = w_in.astype(BF16)
    wa = jnp.pad(wb[:, 2 * GLA_KEY_WIDTH + 2 * GLA_VAL_WIDTH:], ((0, 0), (0, LANES - GLA_RANK)))
    gw = jnp.pad(gate_w2.astype(BF16), ((0, LANES - GLA_RANK), (0, 0)))
    q, k, v, r, la = _proj_odd(x, wb, wa, gw, gate_b.reshape(1, GLA_KEY_WIDTH))
    o = _gla(q, k, la, v, r, norm_g.reshape(1, GLA_DV))
    return _post_odd(o, x, w_out.astype(BF16), *mlp)


def kernel(x, a_w_in, a_cmp_pos, a_cmp_w1, a_cmp_b1, a_cmp_w2, a_pool_w, a_pool_scale, a_w_out, c_w_in, c_gate_w2, c_gate_b, c_norm_g, c_w_out, ln1_g, ln1_b, ln2_g, ln2_b, mlp_w1, mlp_w2):
    S = x.shape[1]
    selt, eb = _selection_constants(S)
    w1_all, w2_all = mlp_w1.astype(BF16), mlp_w2.astype(BF16)
    for i in range(DEPTH):
        j = i // 2
        mlp = (ln1_g[i].reshape(1, D_MODEL), ln1_b[i].reshape(1, D_MODEL), w1_all, w2_all,
               ln2_g[i].reshape(1, D_MODEL), ln2_b[i].reshape(1, D_MODEL), i)
        if i % 2 == 0:
            w = _even_weights(a_w_in[j], a_cmp_pos[j], a_cmp_w1[j], a_cmp_b1[j], a_cmp_w2[j],
                              a_pool_w[j], a_pool_scale[j], a_w_out[j])
            x = _even_layer(x, w, mlp, selt, eb)
        else:
            x = _odd_layer(x, c_w_in[j], c_gate_w2[j], c_gate_b[j], c_norm_g[j], c_w_out[j], mlp)
    return x
```

```python
import functools

import numpy as np
import jax
import jax.numpy as jnp
from jax import lax
from jax.experimental import pallas as pl
from jax.experimental.pallas import tpu as pltpu

F32 = jnp.float32
BF16 = jnp.bfloat16

D_MODEL = 1024
DEPTH = 4
NSA_HEADS = 8
NSA_GROUPS = 2
NSA_HPG = NSA_HEADS // NSA_GROUPS
NSA_DH = 64
NSA_WIDTH = NSA_HEADS * NSA_DH
NSA_KV_WIDTH = NSA_GROUPS * NSA_DH
CMP_BLOCK = 32
CMP_STRIDE = 16
SEL_BLOCK = 64
SEL_TOP_N = 16
WINDOW = 512
POOL_WINDOWS = (2, 4, 8, 16)
POOL_GROUP_DIM = 128
POOL_WIDTH = 512
GLA_HEADS = 4
GLA_DK = 128
GLA_DV = 256
GLA_KEY_WIDTH = GLA_HEADS * GLA_DK
GLA_VAL_WIDTH = GLA_HEADS * GLA_DV
GLA_RANK = 16
GLA_TAU = 16.0
GLA_CHUNK = 64
D_FF = 4 * D_MODEL
ALPHA = (2 * DEPTH) ** 0.25
LN_EPS = 1e-5
MASK_VALUE = -1e30
FORCE_SCORE = 1e4
LOG2_E = 1.4426950408889634

LANES = 128
VMEM_LIMIT_BYTES = 56 * 1024 * 1024

PROJ_TM = 512
POST_TM = 1024
POST_SUB = 512
ATT_TQ = 256
ATT_CK = 512
FF_CHUNK = 512
POOL_HALO = 16
GLA_TILE = 512
GATE_ROWS = 32
PV_ROWS = 80
PROJ_T_ROWS = NSA_HEADS * LANES + 2 * LANES + GATE_ROWS


def _dot(a, b):
    return jnp.dot(a, b, preferred_element_type=F32)


def _dot_nt(a, b):
    return lax.dot_general(a, b, (((1,), (1,)), ((), ())), preferred_element_type=F32)


def _dot_tn(a, b):
    return lax.dot_general(a, b, (((0,), (0,)), ((), ())), preferred_element_type=F32)


def _params(*semantics):
    return pltpu.CompilerParams(dimension_semantics=semantics, vmem_limit_bytes=VMEM_LIMIT_BYTES)


def _exp2_bf16(d):
    return jnp.exp2(d.astype(BF16))


def _layer_norm(z, g, b):
    mu = jnp.mean(z, axis=-1, keepdims=True)
    zc = z - mu
    var = jnp.mean(zc * zc, axis=-1, keepdims=True)
    return zc * lax.rsqrt(var + LN_EPS) * g + b


def _proj_even_kernel(x_ref, wt_ref, wr_ref, eb_ref,
                      qt_ref, vst_ref, vwt_ref, gt_ref, kc_ref, vc_ref, ks_ref, kw_ref, u_ref, cmp_ref):
    xb = x_ref[0].astype(BF16)
    ht = _dot_nt(wt_ref[...], xb)
    nq = NSA_HEADS * LANES
    for h in range(NSA_HEADS):
        qt_ref[0, h] = ht[h * LANES:(h + 1) * LANES].astype(BF16)
    pad_rows = lax.broadcasted_iota(jnp.int32, (PV_ROWS - NSA_DH, ht.shape[1]), 0)
    ones_then_zeros = jnp.where(pad_rows == 0, 1.0, 0.0)
    for g in range(NSA_GROUPS):
        for ref, base in ((vst_ref, nq), (vwt_ref, nq + LANES)):
            v_g = ht[base + g * NSA_DH:base + (g + 1) * NSA_DH]
            ref[0, g] = jnp.concatenate([v_g, ones_then_zeros], axis=0).astype(BF16)
    gt_ref[0] = jax.nn.sigmoid(ht[nq + 2 * LANES:])
    hr = _dot(xb, wr_ref[...])
    n_rows = hr.shape[0] // CMP_STRIDE
    for which, dst in enumerate((kc_ref, vc_ref)):
        cmp_ref[which] = hr[:, which * LANES:(which + 1) * LANES]
        for l in range(CMP_STRIDE):
            dst[0, :, l * LANES:(l + 1) * LANES] = cmp_ref[which, pl.ds(l, n_rows, stride=CMP_STRIDE), :]
    ks_ref[0, :, :LANES] = hr[:, 2 * LANES:3 * LANES].astype(BF16)
    ks_ref[0, :, LANES:] = eb_ref[...]
    kw_ref[0] = hr[:, 3 * LANES:4 * LANES].astype(BF16)
    u_ref[0] = hr[:, 4 * LANES:]


def _proj_even(x, wt, wr, eb):
    B, S, D = x.shape
    tm = PROJ_TM
    row = lambda n: pl.BlockSpec((1, tm, n), lambda b, i: (b, i, 0))
    col = lambda n: pl.BlockSpec((1, n, tm), lambda b, i: (b, 0, i))
    full = lambda a: pl.BlockSpec(a.shape, lambda b, i: (0,) * a.ndim)
    out_shape = (
        jax.ShapeDtypeStruct((B, NSA_HEADS, LANES, S), BF16),
        jax.ShapeDtypeStruct((B, NSA_GROUPS, PV_ROWS, S), BF16),
        jax.ShapeDtypeStruct((B, NSA_GROUPS, PV_ROWS, S), BF16),
        jax.ShapeDtypeStruct((B, GATE_ROWS, S), F32),
        jax.ShapeDtypeStruct((B, S // CMP_STRIDE, CMP_STRIDE * LANES), F32),
        jax.ShapeDtypeStruct((B, S // CMP_STRIDE, CMP_STRIDE * LANES), F32),
        jax.ShapeDtypeStruct((B, S, 2 * LANES), BF16),
        jax.ShapeDtypeStruct((B, S, LANES), BF16),
        jax.ShapeDtypeStruct((B, S, POOL_WIDTH), F32),
    )
    out_specs = (
        pl.BlockSpec((1, NSA_HEADS, LANES, tm), lambda b, i: (b, 0, 0, i)),
        pl.BlockSpec((1, NSA_GROUPS, PV_ROWS, tm), lambda b, i: (b, 0, 0, i)),
        pl.BlockSpec((1, NSA_GROUPS, PV_ROWS, tm), lambda b, i: (b, 0, 0, i)), col(GATE_ROWS),
        pl.BlockSpec((1, tm // CMP_STRIDE, CMP_STRIDE * LANES), lambda b, i: (b, i, 0)),
        pl.BlockSpec((1, tm // CMP_STRIDE, CMP_STRIDE * LANES), lambda b, i: (b, i, 0)),
        row(2 * LANES), row(LANES), row(POOL_WIDTH),
    )
    return pl.pallas_call(
        _proj_even_kernel,
        grid=(B, S // tm),
        in_specs=[row(D), full(wt), full(wr), pl.BlockSpec((tm, LANES), lambda b, i: (i, 0))],
        out_specs=out_specs,
        out_shape=out_shape,
        scratch_shapes=[pltpu.VMEM((2, tm, LANES), F32)],
        compiler_params=_params("parallel", "parallel"),
        name="proj_even",
    )(x, wt, wr, eb)


def _gelu_tanh(x):
    return 0.5 * x * (1.0 + jnp.tanh(0.7978845608028654 * (x + 0.044715 * (x * x * x))))


def _compress_kernel(kc_ref, vc_ref, pos_ref, w1_ref, b1_ref, w2_ref, okc_ref, ovct_ref):
    for which, src in enumerate((kc_ref, vc_ref)):
        xr = src[0]
        nr = xr.shape[0]
        lo = _dot((xr + pos_ref[which, 0]).astype(BF16), w1_ref[which, 0])
        hi = _dot((xr + pos_ref[which, 1]).astype(BF16), w1_ref[which, 1])
        hi_next = pltpu.roll(hi, nr - 1, axis=0)
        act = _gelu_tanh(lo + hi_next + b1_ref[which])
        out = _dot(act.astype(BF16), w2_ref[which])
        rows = lax.broadcasted_iota(jnp.int32, out.shape, 0)
        out = jnp.where(rows < nr - 1, out, 0.0)
        if which == 0:
            okc_ref[0] = out.astype(BF16)
        else:
            ovct_ref[0] = out.T.astype(BF16)


def _compress(kc, vc, pos, w1, b1, w2):
    B, NR, W = kc.shape
    blk = pl.BlockSpec((1, NR, W), lambda b: (b, 0, 0))
    full = lambda a: pl.BlockSpec(a.shape, lambda b: (0,) * a.ndim)
    return pl.pallas_call(
        _compress_kernel,
        grid=(B,),
        in_specs=[blk, blk, full(pos), full(w1), full(b1), full(w2)],
        out_specs=(pl.BlockSpec((1, NR, LANES), lambda b: (b, 0, 0)),
                   pl.BlockSpec((1, LANES, NR), lambda b: (b, 0, 0))),
        out_shape=(jax.ShapeDtypeStruct((B, NR, LANES), BF16),
                   jax.ShapeDtypeStruct((B, LANES, NR), BF16)),
        compiler_params=_params("parallel"),
        name="compress",
    )(kc, vc, pos, w1, b1, w2)


def _nsa_kernel(qt_ref, kc_ref, vct_ref, ks_ref, vst_ref, kw_ref, vwt_ref, gt_ref, selt_ref, o_ref,
                qsel_ref, oc_ref, s_ref, sw_ref, acc_ref, *, seq_len):
    tq = ATT_TQ
    cols = NSA_HPG * tq
    n_sel = seq_len // SEL_BLOCK
    t0 = pl.program_id(1) * tq
    t_q = t0 + lax.broadcasted_iota(jnp.int32, (1, tq), 1)
    t_cols = jnp.concatenate([t_q] * NSA_HPG, axis=1)

    for g in range(NSA_GROUPS):
        qt = jnp.concatenate([qt_ref[0, NSA_HPG * g + h] for h in range(NSA_HPG)], axis=1)

        s_c = _dot(kc_ref[0], qt)
        c_end = lax.broadcasted_iota(jnp.int32, s_c.shape, 0) * CMP_STRIDE + (CMP_BLOCK - 1)
        valid_c = c_end <= t_cols
        s_c = jnp.where(valid_c, s_c, MASK_VALUE)
        e_c = jnp.where(valid_c, jnp.exp2(s_c - jnp.max(s_c, axis=0, keepdims=True)), 0.0)
        p_c = e_c * (1.0 / jnp.maximum(jnp.sum(e_c, axis=0, keepdims=True), 1e-30))
        oc_ref[g] = _dot(vct_ref[0], p_c.astype(BF16))[g * NSA_DH:(g + 1) * NSA_DH]

        p_sum = p_c[:, 0:tq]
        for h in range(1, NSA_HPG):
            p_sum = p_sum + p_c[:, h * tq:(h + 1) * tq]
        p_hi = p_sum.astype(BF16)
        p_lo = (p_sum - p_hi.astype(F32)).astype(BF16)
        imp = _dot(selt_ref[...], p_hi) + _dot(selt_ref[...], p_lo)
        blk = lax.broadcasted_iota(jnp.int32, imp.shape, 0)
        cur = t_q >> 6
        forced = (blk == 0) | (blk == cur) | (blk == cur - 1)
        score = jnp.where(forced, FORCE_SCORE, jnp.where(blk > cur, -FORCE_SCORE, imp))
        beaten = jnp.zeros(imp.shape, F32)
        for i in range(n_sel):
            s_i = score[i:i + 1, :]
            wins = (s_i > score) | ((s_i == score) & (blk > i))
            beaten = beaten + jnp.where(wins, 1.0, 0.0)
        not_sel = jnp.where(beaten < float(SEL_TOP_N), 0.0, 1.0)
        not_sel = jnp.concatenate([not_sel, jnp.zeros((LANES - n_sel, tq), F32)], axis=0).astype(BF16)
        qsel_ref[g, 0:LANES, :] = qt
        qsel_ref[g, LANES:, :] = jnp.concatenate([not_sel] * NSA_HPG, axis=1)

    def score(c, maxes, on_diagonal):
        k0 = pl.multiple_of(c * ATT_CK, ATT_CK)
        k_blk = ks_ref[0, pl.ds(k0, ATT_CK), :]
        out = []
        for g in range(NSA_GROUPS):
            s = _dot(k_blk, qsel_ref[g])
            if on_diagonal:
                kpos = k0 + lax.broadcasted_iota(jnp.int32, s.shape, 0)
                s = jnp.where(kpos <= t_cols, s, MASK_VALUE)
            s_ref[g, pl.ds(k0, ATT_CK), :] = s
            out.append(jnp.maximum(maxes[g], jnp.max(s, axis=0, keepdims=True)))
        return tuple(out)

    last = (t0 + tq - 1) // ATT_CK
    m_init = jnp.full((1, cols), MASK_VALUE, F32)
    maxes = lax.fori_loop(0, last, lambda c, m: score(c, m, False), (m_init,) * NSA_GROUPS)
    maxes = score(last, maxes, True)

    w0 = pl.multiple_of(jnp.maximum(t0 - WINDOW, 0), tq)
    wlen = WINDOW + tq
    wpos = w0 + lax.broadcasted_iota(jnp.int32, (wlen, cols), 0)
    in_window = (wpos <= t_cols) & (wpos > t_cols - WINDOW)
    k_win = kw_ref[0, pl.ds(w0, wlen), :]
    win_maxes = []
    for g in range(NSA_GROUPS):
        s_w = jnp.where(in_window, _dot(k_win, qsel_ref[g, 0:LANES, :]), MASK_VALUE)
        sw_ref[g] = s_w
        win_maxes.append(jnp.max(s_w, axis=0, keepdims=True))

    acc_ref[...] = jnp.zeros(acc_ref.shape, F32)

    def attend(c, carry):
        k0 = pl.multiple_of(c * ATT_CK, ATT_CK)
        for g in range(NSA_GROUPS):
            p = _exp2_bf16(s_ref[g, pl.ds(k0, ATT_CK), :] - maxes[g])
            acc_ref[g] += _dot(vst_ref[0, g, :, pl.ds(k0, ATT_CK)], p)
        return carry

    lax.fori_loop(0, last, attend, 0)
    attend(last, 0)
    acc_win = [_dot(vwt_ref[0, g, :, pl.ds(w0, wlen)], _exp2_bf16(sw_ref[g] - win_maxes[g]))
               for g in range(NSA_GROUPS)]

    gates = gt_ref[0]
    for g in range(NSA_GROUPS):
        acc_s = acc_ref[g]
        o_s = acc_s[0:NSA_DH] * (1.0 / jnp.maximum(acc_s[NSA_DH:NSA_DH + 1], 1e-30))
        o_w = acc_win[g][0:NSA_DH] * (1.0 / jnp.maximum(acc_win[g][NSA_DH:NSA_DH + 1], 1e-30))

        o_c = oc_ref[g]
        heads = []
        for h in range(NSA_HPG):
            c = slice(h * tq, (h + 1) * tq)
            gi = (NSA_HPG * g + h) * 3
            mix = (gates[gi:gi + 1] * o_c[:, c] + gates[gi + 1:gi + 2] * o_s[:, c]
                   + gates[gi + 2:gi + 3] * o_w[:, c])
            heads.append(mix)
        for j in range(NSA_HPG // 2):
            pair = jnp.concatenate([heads[2 * j], heads[2 * j + 1]], axis=0)
            c0 = (NSA_HPG * g + 2 * j) * NSA_DH
            o_ref[0, :, c0:c0 + LANES] = pair.T.astype(BF16)


def _nsa_attention(qt, kc, vct, ks, vst, kw, vwt, gt, selt):
    B, _, _, S = qt.shape
    tq = ATT_TQ
    cols = NSA_HPG * tq
    seq_rows = lambda n: pl.BlockSpec((1, S, n), lambda b, i: (b, 0, 0))
    seq_cols = pl.BlockSpec((1, NSA_GROUPS, PV_ROWS, S), lambda b, i: (b, 0, 0, 0))
    whole = lambda a: pl.BlockSpec((1,) + a.shape[1:], lambda b, i: (b, 0, 0))
    return pl.pallas_call(
        functools.partial(_nsa_kernel, seq_len=S),
        grid=(B, S // tq),
        in_specs=[pl.BlockSpec((1, NSA_HEADS, LANES, tq), lambda b, i: (b, 0, 0, i)),
                  whole(kc), whole(vct), seq_rows(2 * LANES), seq_cols, seq_rows(LANES),
                  seq_cols, pl.BlockSpec((1, GATE_ROWS, tq), lambda b, i: (b, 0, i)),
                  pl.BlockSpec(selt.shape, lambda b, i: (0, 0))],
        out_specs=pl.BlockSpec((1, tq, NSA_WIDTH), lambda b, i: (b, i, 0)),
        out_shape=jax.ShapeDtypeStruct((B, S, NSA_WIDTH), BF16),
        scratch_shapes=[pltpu.VMEM((NSA_GROUPS, 2 * LANES, cols), BF16),
                        pltpu.VMEM((NSA_GROUPS, NSA_DH, cols), F32),
                        pltpu.VMEM((NSA_GROUPS, S, cols), F32),
                        pltpu.VMEM((NSA_GROUPS, WINDOW + tq, cols), F32),
                        pltpu.VMEM((NSA_GROUPS, PV_ROWS, cols), F32)],
        compiler_params=_params("parallel", "parallel"),
        name="nsa_attention",
    )(qt, kc, vct, ks, vst, kw, vwt, gt, selt)


def _norm_mlp_norm(mix_fns, x_ref, y_ref, g1_ref, b1_ref, w1_ref, w2_ref, g2_ref, b2_ref, x1_ref, xb_ref, acc_ref):
    tiles = [slice(i * POST_SUB, (i + 1) * POST_SUB) for i in range(len(mix_fns))]
    n_ff = D_FF // FF_CHUNK

    def norm1(i):
        rows = tiles[i]
        x1_ref[rows] = _layer_norm(ALPHA * x_ref[0, rows] + mix_fns[i](), g1_ref[...], b1_ref[...])
        xb_ref[rows] = x1_ref[rows].astype(BF16)

    def mlp(i, chunks):
        rows = tiles[i]
        for c in chunks:
            cols = slice(c * FF_CHUNK, (c + 1) * FF_CHUNK)
            h = jnp.maximum(_dot(xb_ref[rows], w1_ref[:, cols]), 0.0)
            part = _dot((h * h).astype(BF16), w2_ref[cols, :])
            if c == 0:
                acc_ref[rows] = part
            else:
                acc_ref[rows] += part

    def norm2(i):
        rows = tiles[i]
        y_ref[0, rows] = _layer_norm(ALPHA * x1_ref[rows] + acc_ref[rows], g2_ref[...], b2_ref[...])

    first, second = range(0, n_ff // 2), range(n_ff // 2, n_ff)
    norm1(0)
    for i in range(len(tiles)):
        mlp(i, first)
        if i > 0:
            norm2(i - 1)
        if i + 1 < len(tiles):
            norm1(i + 1)
        mlp(i, second)
    norm2(len(tiles) - 1)


def _post_scratch(tm):
    return [pltpu.VMEM((tm, D_MODEL), F32), pltpu.VMEM((tm, D_MODEL), BF16), pltpu.VMEM((tm, D_MODEL), F32)]


def _post_even_kernel(o_ref, u_ref, x_ref, wo_ref, pw_ref, ps_ref, g1_ref, b1_ref, w1_ref, w2_ref,
                      g2_ref, b2_ref, y_ref, ext_ref, cat_ref, x1_ref, xb_ref, acc_ref):
    tm = u_ref.shape[1]
    i = pl.program_id(1)

    @pl.when(i == 0)
    def _():
        ext_ref[0:POOL_HALO, :] = jnp.zeros((POOL_HALO, POOL_WIDTH), F32)

    ext_ref[POOL_HALO:, :] = u_ref[0]

    def mix(j):
        r0 = j * POST_SUB
        rows = slice(r0, r0 + POST_SUB)
        t = (i * tm + r0 + lax.broadcasted_iota(jnp.int32, (POST_SUB, 1), 0) + 1).astype(F32)
        cat_ref[rows, 0:NSA_WIDTH] = o_ref[0, rows]
        for gi, w in enumerate(POOL_WINDOWS):
            cols = slice(gi * POOL_GROUP_DIM, (gi + 1) * POOL_GROUP_DIM)
            u_g = u_ref[0, rows, cols]
            acc = u_g
            for d in range(1, w):
                acc = acc + ext_ref[POOL_HALO + r0 - d:POOL_HALO + r0 - d + POST_SUB, cols]
            r = acc / jnp.minimum(t, float(w)) - u_g
            y_g = _dot(r.astype(BF16), pw_ref[gi]) * ps_ref[:, cols]
            cat_ref[rows, NSA_WIDTH + gi * POOL_GROUP_DIM:NSA_WIDTH + (gi + 1) * POOL_GROUP_DIM] = y_g.astype(BF16)
        return _dot(cat_ref[rows], wo_ref[...])

    mix_fns = [functools.partial(mix, j) for j in range(tm // POST_SUB)]
    _norm_mlp_norm(mix_fns, x_ref, y_ref, g1_ref, b1_ref, w1_ref, w2_ref, g2_ref, b2_ref,
                   x1_ref, xb_ref, acc_ref)
    ext_ref[0:POOL_HALO, :] = u_ref[0, tm - POOL_HALO:, :]


def _layer_weight_spec(stacked, layer):
    return pl.BlockSpec((None,) + stacked.shape[1:], lambda bb, i: (layer,) + (0,) * (stacked.ndim - 1),
                        pipeline_mode=pl.Buffered(1))


def _post_even(o, u, x, wo, pw, ps, g1, b1, w1, w2, g2, b2, layer):
    B, S, D = x.shape
    tm = POST_TM
    row = lambda n: pl.BlockSpec((1, tm, n), lambda bb, i: (bb, i, 0))
    full = lambda a: pl.BlockSpec(a.shape, lambda bb, i: (0,) * a.ndim)
    return pl.pallas_call(
        _post_even_kernel,
        grid=(B, S // tm),
        in_specs=[row(NSA_WIDTH), row(POOL_WIDTH), row(D), full(wo), full(pw), full(ps), full(g1), full(b1),
                  _layer_weight_spec(w1, layer), _layer_weight_spec(w2, layer), full(g2), full(b2)],
        out_specs=row(D),
        out_shape=jax.ShapeDtypeStruct((B, S, D), F32),
        scratch_shapes=[pltpu.VMEM((POOL_HALO + tm, POOL_WIDTH), F32),
                        pltpu.VMEM((tm, NSA_WIDTH + POOL_WIDTH), BF16)] + _post_scratch(tm),
        compiler_params=_params("parallel", "arbitrary"),
        name="post_even",
    )(o, u, x, wo, pw, ps, g1, b1, w1, w2, g2, b2)


def _proj_odd_kernel(x_ref, w_ref, wa_ref, gw_ref, gb_ref, q_ref, k_ref, v_ref, r_ref, la_ref):
    xb = x_ref[0].astype(BF16)
    a = _dot(xb, wa_ref[...])
    z = _dot(a.astype(BF16), gw_ref[...]) + gb_ref[...]
    log_sig = jnp.minimum(z, 0.0) - jnp.log(1.0 + jnp.exp(-jnp.abs(z)))
    la_ref[0] = log_sig / GLA_TAU
    n = FF_CHUNK
    base = 0
    for ref, width in ((q_ref, GLA_KEY_WIDTH), (k_ref, GLA_KEY_WIDTH), (v_ref, GLA_VAL_WIDTH),
                       (r_ref, GLA_VAL_WIDTH)):
        for c in range(width // n):
            ref[0, :, c * n:(c + 1) * n] = _dot(xb, w_ref[:, base + c * n:base + (c + 1) * n]).astype(ref.dtype)
        base += width


def _proj_odd(x, w, wa, gw, gb):
    B, S, D = x.shape
    tm = PROJ_TM
    row = lambda n: pl.BlockSpec((1, tm, n), lambda b, i: (b, i, 0))
    full = lambda a: pl.BlockSpec(a.shape, lambda b, i: (0,) * a.ndim)
    return pl.pallas_call(
        _proj_odd_kernel,
        grid=(B, S // tm),
        in_specs=[row(D), full(w), full(wa), full(gw), full(gb)],
        out_specs=(row(GLA_KEY_WIDTH), row(GLA_KEY_WIDTH), row(GLA_VAL_WIDTH), row(GLA_VAL_WIDTH),
                   row(GLA_KEY_WIDTH)),
        out_shape=(jax.ShapeDtypeStruct((B, S, GLA_KEY_WIDTH), F32),
                   jax.ShapeDtypeStruct((B, S, GLA_KEY_WIDTH), F32),
                   jax.ShapeDtypeStruct((B, S, GLA_VAL_WIDTH), BF16),
                   jax.ShapeDtypeStruct((B, S, GLA_VAL_WIDTH), F32),
                   jax.ShapeDtypeStruct((B, S, GLA_KEY_WIDTH), F32)),
        compiler_params=_params("parallel", "parallel"),
        name="proj_odd",
    )(x, w, wa, gw, gb)


def _gla_kernel(q_ref, k_ref, la_ref, v_ref, r_ref, ng_ref, o_ref, state_ref):
    C = GLA_CHUNK

    @pl.when(pl.program_id(1) == 0)
    def _():
        state_ref[...] = jnp.zeros(state_ref.shape, F32)

    row = lax.broadcasted_iota(jnp.int32, (C, GLA_DK), 0)
    causal = (lax.broadcasted_iota(jnp.int32, (C, C), 0) >= lax.broadcasted_iota(jnp.int32, (C, C), 1))
    for n in range(q_ref.shape[1] // C):
        rows = slice(n * C, (n + 1) * C)
        for h in range(GLA_HEADS):
            kcols = slice(h * GLA_DK, (h + 1) * GLA_DK)
            vcols = slice(h * GLA_DV, (h + 1) * GLA_DV)
            b = la_ref[0, rows, kcols]
            shift = 1
            while shift < C:
                b = b + jnp.where(row >= shift, pltpu.roll(b, shift, axis=0), 0.0)
                shift *= 2
            b_last = b[C - 1:C, :]
            q_t = (q_ref[0, rows, kcols] * (GLA_DK ** -0.5)) * jnp.exp(b)
            k = k_ref[0, rows, kcols]
            k_t = k * jnp.exp(-b)
            k_u = k * jnp.exp(b_last - b)
            v = v_ref[0, rows, vcols]
            q_tb = q_t.astype(BF16)
            att = jnp.where(causal, _dot_nt(q_tb, k_t.astype(BF16)), 0.0)
            state = state_ref[h]
            o = _dot(att.astype(BF16), v) + _dot_nt(q_tb, state.astype(BF16))
            upd = _dot_tn(v, k_u.astype(BF16))
            state_ref[h] = jnp.exp(b_last) * state + upd
            o = o * lax.rsqrt(jnp.mean(o * o, axis=-1, keepdims=True) + LN_EPS) * ng_ref[...]
            o_ref[0, rows, vcols] = (o * jax.nn.silu(r_ref[0, rows, vcols])).astype(BF16)


def _gla(q, k, la, v, r, ng):
    B, S, _ = q.shape
    row = lambda n: pl.BlockSpec((1, GLA_TILE, n), lambda b, i: (b, i, 0))
    return pl.pallas_call(
        _gla_kernel,
        grid=(B, S // GLA_TILE),
        in_specs=[row(GLA_KEY_WIDTH), row(GLA_KEY_WIDTH), row(GLA_KEY_WIDTH), row(GLA_VAL_WIDTH),
                  row(GLA_VAL_WIDTH), pl.BlockSpec(ng.shape, lambda b, i: (0, 0))],
        out_specs=row(GLA_VAL_WIDTH),
        out_shape=jax.ShapeDtypeStruct((B, S, GLA_VAL_WIDTH), BF16),
        scratch_shapes=[pltpu.VMEM((GLA_HEADS, GLA_DV, GLA_DK), F32)],
        compiler_params=_params("parallel", "arbitrary"),
        name="gla",
    )(q, k, la, v, r, ng)


def _post_odd_kernel(o_ref, x_ref, wo_ref, g1_ref, b1_ref, w1_ref, w2_ref, g2_ref, b2_ref, y_ref,
                     x1_ref, xb_ref, acc_ref):
    def mix(i):
        return _dot(o_ref[0, i * POST_SUB:(i + 1) * POST_SUB], wo_ref[...])

    mix_fns = [functools.partial(mix, i) for i in range(o_ref.shape[1] // POST_SUB)]
    _norm_mlp_norm(mix_fns, x_ref, y_ref, g1_ref, b1_ref, w1_ref, w2_ref, g2_ref, b2_ref,
                   x1_ref, xb_ref, acc_ref)


def _post_odd(o, x, wo, g1, b1, w1, w2, g2, b2, layer):
    B, S, D = x.shape
    tm = POST_TM
    row = lambda n: pl.BlockSpec((1, tm, n), lambda bb, i: (bb, i, 0))
    full = lambda a: pl.BlockSpec(a.shape, lambda bb, i: (0,) * a.ndim)
    return pl.pallas_call(
        _post_odd_kernel,
        grid=(B, S // tm),
        in_specs=[row(GLA_VAL_WIDTH), row(D), full(wo), full(g1), full(b1), _layer_weight_spec(w1, layer),
                  _layer_weight_spec(w2, layer), full(g2), full(b2)],
        out_specs=row(D),
        out_shape=jax.ShapeDtypeStruct((B, S, D), F32),
        scratch_shapes=_post_scratch(tm),
        compiler_params=_params("parallel", "parallel"),
        name="post_odd",
    )(o, x, wo, g1, b1, w1, w2, g2, b2)


def _even_weights(w_in, cmp_pos, cmp_w1, cmp_b1, cmp_w2, pool_w, pool_scale, w_out):
    o1 = NSA_WIDTH
    o2 = o1 + 6 * NSA_KV_WIDTH
    o3 = o2 + 3 * NSA_HEADS
    D = w_in.shape[0]
    wq = (w_in[:, :o1] * (NSA_DH ** -0.5 * LOG2_E)).reshape(D, NSA_GROUPS, NSA_HPG, NSA_DH)
    slots = jnp.zeros((D, NSA_GROUPS, NSA_HPG, NSA_GROUPS, NSA_DH), F32)
    for g in range(NSA_GROUPS):
        slots = slots.at[:, g, :, g, :].set(wq[:, g])
    wq = slots.reshape(D, NSA_HEADS * LANES)
    kv = [w_in[:, o1 + n * NSA_KV_WIDTH:o1 + (n + 1) * NSA_KV_WIDTH] for n in range(6)]
    k_cmp, v_cmp, k_slc, v_slc, k_win, v_win = kv
    wg = jnp.pad(w_in[:, o2:o3], ((0, 0), (0, GATE_ROWS - 3 * NSA_HEADS)))
    wt = jnp.concatenate([wq, v_slc, v_win, wg], axis=1).T.astype(BF16)
    wr = jnp.concatenate([k_cmp, v_cmp, k_slc, k_win, w_in[:, o3:]], axis=1).astype(BF16)
    half = CMP_BLOCK // 2
    eye = jnp.eye(NSA_GROUPS, dtype=F32)
    w1 = cmp_w1.reshape(2, 2, half, NSA_DH, NSA_DH)
    w1 = jnp.einsum('whldo,gk->whlgdko', w1, eye).reshape(2, 2, half * LANES, LANES).astype(BF16)
    pos = cmp_pos.reshape(2, 2, half, 1, NSA_DH)
    pos = jnp.broadcast_to(pos, (2, 2, half, NSA_GROUPS, NSA_DH)).reshape(2, 2, 1, half * LANES)
    b1 = jnp.tile(cmp_b1, (1, NSA_GROUPS)).reshape(2, 1, LANES)
    w2 = jnp.einsum('wdo,gk->wgdko', cmp_w2, eye).reshape(2, LANES, LANES).astype(BF16)
    return dict(wt=wt, wr=wr, pos=pos, w1=w1, b1=b1, w2=w2,
                pw=pool_w.astype(BF16), ps=pool_scale.reshape(1, POOL_WIDTH), wo=w_out.astype(BF16))


def _selection_constants(seq_len):
    n_cmp = (seq_len - CMP_BLOCK) // CMP_STRIDE + 1
    n_sel = seq_len // SEL_BLOCK
    sub = np.arange(n_cmp)[:, None] + np.arange(CMP_BLOCK // CMP_STRIDE)[None, :]
    owner = sub // (SEL_BLOCK // CMP_STRIDE)
    sel_map = (owner[:, :, None] == np.arange(n_sel)[None, None, :]).sum(1).astype(np.float32)
    selt = np.zeros((n_sel, seq_len // CMP_STRIDE), np.float32)
    selt[:, :n_cmp] = sel_map.T
    eb = np.zeros((seq_len, LANES), np.float32)
    eb[np.arange(seq_len), np.arange(seq_len) // SEL_BLOCK] = MASK_VALUE
    return jnp.asarray(selt, BF16), jnp.asarray(eb, BF16)


def _even_layer(x, w, mlp, selt, eb):
    B, S, _ = x.shape
    qt, vst, vwt, gt, kc, vc, ks, kw, u = _proj_even(x, w['wt'], w['wr'], eb)
    kcc, vcct = _compress(kc, vc, w['pos'], w['w1'], w['b1'], w['w2'])
    o = _nsa_attention(qt, kcc, vcct, ks, vst, kw, vwt, gt, selt)
    return _post_even(o, u, x, w['wo'], w['pw'], w['ps'], *mlp)


def _odd_layer(x, w_in, gate_w2, gate_b, norm_g, w_out, mlp):
    wb = w_in.astype(BF16)
    wa = jnp.pad(wb[:, 2 * GLA_KEY_WIDTH + 2 * GLA_VAL_WIDTH:], ((0, 0), (0, LANES - GLA_RANK)))
    gw = jnp.pad(gate_w2.astype(BF16), ((0, LANES - GLA_RANK), (0, 0)))
    q, k, v, r, la = _proj_odd(x, wb, wa, gw, gate_b.reshape(1, GLA_KEY_WIDTH))
    o = _gla(q, k, la, v, r, norm_g.reshape(1, GLA_DV))
    return _post_odd(o, x, w_out.astype(BF16), *mlp)


def kernel(x, a_w_in, a_cmp_pos, a_cmp_w1, a_cmp_b1, a_cmp_w2, a_pool_w, a_pool_scale, a_w_out, c_w_in, c_gate_w2, c_gate_b, c_norm_g, c_w_out, ln1_g, ln1_b, ln2_g, ln2_b, mlp_w1, mlp_w2):
    S = x.shape[1]
    selt, eb = _selection_constants(S)
    w1_all, w2_all = mlp_w1.astype(BF16), mlp_w2.astype(BF16)
    for i in range(DEPTH):
        j = i // 2
        mlp = (ln1_g[i].reshape(1, D_MODEL), ln1_b[i].reshape(1, D_MODEL), w1_all, w2_all,
               ln2_g[i].reshape(1, D_MODEL), ln2_b[i].reshape(1, D_MODEL), i)
        if i % 2 == 0:
            w = _even_weights(a_w_in[j], a_cmp_pos[j], a_cmp_w1[j], a_cmp_b1[j], a_cmp_w2[j],
                              a_pool_w[j], a_pool_scale[j], a_w_out[j])
            x = _even_layer(x, w, mlp, selt, eb)
        else:
            x = _odd_layer(x, c_w_in[j], c_gate_w2[j], c_gate_b[j], c_norm_g[j], c_w_out[j], mlp)
    return x
```

```python
import functools

import numpy as np
import jax
import jax.numpy as jnp
from jax import lax
from jax.experimental import pallas as pl
from jax.experimental.pallas import tpu as pltpu

F32 = jnp.float32
BF16 = jnp.bfloat16

D_MODEL = 1024
DEPTH = 4
NSA_HEADS = 8
NSA_GROUPS = 2
NSA_HPG = NSA_HEADS // NSA_GROUPS
NSA_DH = 64
NSA_WIDTH = NSA_HEADS * NSA_DH
NSA_KV_WIDTH = NSA_GROUPS * NSA_DH
CMP_BLOCK = 32
CMP_STRIDE = 16
SEL_BLOCK = 64
SEL_TOP_N = 16
WINDOW = 512
POOL_WINDOWS = (2, 4, 8, 16)
POOL_GROUP_DIM = 128
POOL_WIDTH = 512
GLA_HEADS = 4
GLA_DK = 128
GLA_DV = 256
GLA_KEY_WIDTH = GLA_HEADS * GLA_DK
GLA_VAL_WIDTH = GLA_HEADS * GLA_DV
GLA_RANK = 16
GLA_TAU = 16.0
GLA_CHUNK = 64
D_FF = 4 * D_MODEL
ALPHA = (2 * DEPTH) ** 0.25
LN_EPS = 1e-5
MASK_VALUE = -1e30
FORCE_SCORE = 1e4
LOG2_E = 1.4426950408889634

LANES = 128
VMEM_LIMIT_BYTES = 56 * 1024 * 1024

PROJ_TM = 1024
POST_TM = 1024
POST_SUB = 512
ATT_TQ = 256
ATT_CK = 512
FF_CHUNK = 512
POOL_HALO = 16
GLA_TILE = 512
GATE_ROWS = 32
PV_ROWS = 80
PROJ_T_ROWS = NSA_HEADS * LANES + 2 * LANES + GATE_ROWS


def _dot(a, b):
    return jnp.dot(a, b, preferred_element_type=F32)


def _dot_nt(a, b):
    return lax.dot_general(a, b, (((1,), (1,)), ((), ())), preferred_element_type=F32)


def _dot_tn(a, b):
    return lax.dot_general(a, b, (((0,), (0,)), ((), ())), preferred_element_type=F32)


def _params(*semantics):
    return pltpu.CompilerParams(dimension_semantics=semantics, vmem_limit_bytes=VMEM_LIMIT_BYTES)


def _exp2_bf16(d):
    return jnp.exp2(d.astype(BF16))


def _layer_norm(z, g, b):
    mu = jnp.mean(z, axis=-1, keepdims=True)
    zc = z - mu
    var = jnp.mean(zc * zc, axis=-1, keepdims=True)
    return zc * lax.rsqrt(var + LN_EPS) * g + b


def _proj_even_kernel(x_ref, wt_ref, wr_ref, eb_ref,
                      qt_ref, vst_ref, vwt_ref, gt_ref, kc_ref, vc_ref, ks_ref, kw_ref, u_ref, cmp_ref):
    xb = x_ref[0].astype(BF16)
    ht = _dot_nt(wt_ref[...], xb)
    nq = NSA_HEADS * LANES
    for h in range(NSA_HEADS):
        qt_ref[0, h] = ht[h * LANES:(h + 1) * LANES].astype(BF16)
    pad_rows = lax.broadcasted_iota(jnp.int32, (PV_ROWS - NSA_DH, ht.shape[1]), 0)
    ones_then_zeros = jnp.where(pad_rows == 0, 1.0, 0.0)
    for g in range(NSA_GROUPS):
        for ref, base in ((vst_ref, nq), (vwt_ref, nq + LANES)):
            v_g = ht[base + g * NSA_DH:base + (g + 1) * NSA_DH]
            ref[0, g] = jnp.concatenate([v_g, ones_then_zeros], axis=0).astype(BF16)
    gt_ref[0] = jax.nn.sigmoid(ht[nq + 2 * LANES:])
    hr = _dot(xb, wr_ref[...])
    n_rows = hr.shape[0] // CMP_STRIDE
    for which, dst in enumerate((kc_ref, vc_ref)):
        cmp_ref[which] = hr[:, which * LANES:(which + 1) * LANES]
        for l in range(CMP_STRIDE):
            dst[0, :, l * LANES:(l + 1) * LANES] = cmp_ref[which, pl.ds(l, n_rows, stride=CMP_STRIDE), :]
    ks_ref[0, :, :LANES] = hr[:, 2 * LANES:3 * LANES].astype(BF16)
    ks_ref[0, :, LANES:] = eb_ref[...]
    kw_ref[0] = hr[:, 3 * LANES:4 * LANES].astype(BF16)
    u_ref[0] = hr[:, 4 * LANES:]


def _proj_even(x, wt, wr, eb):
    B, S, D = x.shape
    tm = PROJ_TM
    row = lambda n: pl.BlockSpec((1, tm, n), lambda b, i: (b, i, 0))
    col = lambda n: pl.BlockSpec((1, n, tm), lambda b, i: (b, 0, i))
    full = lambda a: pl.BlockSpec(a.shape, lambda b, i: (0,) * a.ndim)
    out_shape = (
        jax.ShapeDtypeStruct((B, NSA_HEADS, LANES, S), BF16),
        jax.ShapeDtypeStruct((B, NSA_GROUPS, PV_ROWS, S), BF16),
        jax.ShapeDtypeStruct((B, NSA_GROUPS, PV_ROWS, S), BF16),
        jax.ShapeDtypeStruct((B, GATE_ROWS, S), F32),
        jax.ShapeDtypeStruct((B, S // CMP_STRIDE, CMP_STRIDE * LANES), F32),
        jax.ShapeDtypeStruct((B, S // CMP_STRIDE, CMP_STRIDE * LANES), F32),
        jax.ShapeDtypeStruct((B, S, 2 * LANES), BF16),
        jax.ShapeDtypeStruct((B, S, LANES), BF16),
        jax.ShapeDtypeStruct((B, S, POOL_WIDTH), F32),
    )
    out_specs = (
        pl.BlockSpec((1, NSA_HEADS, LANES, tm), lambda b, i: (b, 0, 0, i)),
        pl.BlockSpec((1, NSA_GROUPS, PV_ROWS, tm), lambda b, i: (b, 0, 0, i)),
        pl.BlockSpec((1, NSA_GROUPS, PV_ROWS, tm), lambda b, i: (b, 0, 0, i)), col(GATE_ROWS),
        pl.BlockSpec((1, tm // CMP_STRIDE, CMP_STRIDE * LANES), lambda b, i: (b, i, 0)),
        pl.BlockSpec((1, tm // CMP_STRIDE, CMP_STRIDE * LANES), lambda b, i: (b, i, 0)),
        row(2 * LANES), row(LANES), row(POOL_WIDTH),
    )
    return pl.pallas_call(
        _proj_even_kernel,
        grid=(B, S // tm),
        in_specs=[row(D), full(wt), full(wr), pl.BlockSpec((tm, LANES), lambda b, i: (i, 0))],
        out_specs=out_specs,
        out_shape=out_shape,
        scratch_shapes=[pltpu.VMEM((2, tm, LANES), F32)],
        compiler_params=_params("parallel", "parallel"),
        name="proj_even",
    )(x, wt, wr, eb)


def _gelu_tanh(x):
    return 0.5 * x * (1.0 + jnp.tanh(0.7978845608028654 * (x + 0.044715 * (x * x * x))))


def _compress_kernel(kc_ref, vc_ref, pos_ref, w1_ref, b1_ref, w2_ref, okc_ref, ovct_ref):
    for which, src in enumerate((kc_ref, vc_ref)):
        xr = src[0]
        nr = xr.shape[0]
        lo = _dot((xr + pos_ref[which, 0]).astype(BF16), w1_ref[which, 0])
        hi = _dot((xr + pos_ref[which, 1]).astype(BF16), w1_ref[which, 1])
        hi_next = pltpu.roll(hi, nr - 1, axis=0)
        act = _gelu_tanh(lo + hi_next + b1_ref[which])
        out = _dot(act.astype(BF16), w2_ref[which])
        rows = lax.broadcasted_iota(jnp.int32, out.shape, 0)
        out = jnp.where(rows < nr - 1, out, 0.0)
        if which == 0:
            okc_ref[0] = out.astype(BF16)
        else:
            ovct_ref[0] = out.T.astype(BF16)


def _compress(kc, vc, pos, w1, b1, w2):
    B, NR, W = kc.shape
    blk = pl.BlockSpec((1, NR, W), lambda b: (b, 0, 0))
    full = lambda a: pl.BlockSpec(a.shape, lambda b: (0,) * a.ndim)
    return pl.pallas_call(
        _compress_kernel,
        grid=(B,),
        in_specs=[blk, blk, full(pos), full(w1), full(b1), full(w2)],
        out_specs=(pl.BlockSpec((1, NR, LANES), lambda b: (b, 0, 0)),
                   pl.BlockSpec((1, LANES, NR), lambda b: (b, 0, 0))),
        out_shape=(jax.ShapeDtypeStruct((B, NR, LANES), BF16),
                   jax.ShapeDtypeStruct((B, LANES, NR), BF16)),
        compiler_params=_params("parallel"),
        name="compress",
    )(kc, vc, pos, w1, b1, w2)


def _nsa_kernel(qt_ref, kc_ref, vct_ref, ks_ref, vst_ref, kw_ref, vwt_ref, gt_ref, selt_ref, o_ref,
                qsel_ref, oc_ref, s_ref, sw_ref, acc_ref, *, seq_len):
    tq = ATT_TQ
    cols = NSA_HPG * tq
    n_sel = seq_len // SEL_BLOCK
    t0 = pl.program_id(1) * tq
    t_q = t0 + lax.broadcasted_iota(jnp.int32, (1, tq), 1)
    t_cols = jnp.concatenate([t_q] * NSA_HPG, axis=1)

    for g in range(NSA_GROUPS):
        qt = jnp.concatenate([qt_ref[0, NSA_HPG * g + h] for h in range(NSA_HPG)], axis=1)

        s_c = _dot(kc_ref[0], qt)
        c_end = lax.broadcasted_iota(jnp.int32, s_c.shape, 0) * CMP_STRIDE + (CMP_BLOCK - 1)
        valid_c = c_end <= t_cols
        s_c = jnp.where(valid_c, s_c, MASK_VALUE)
        e_c = jnp.where(valid_c, jnp.exp2(s_c - jnp.max(s_c, axis=0, keepdims=True)), 0.0)
        p_c = e_c * (1.0 / jnp.maximum(jnp.sum(e_c, axis=0, keepdims=True), 1e-30))
        oc_ref[g] = _dot(vct_ref[0], p_c.astype(BF16))[g * NSA_DH:(g + 1) * NSA_DH]

        p_sum = p_c[:, 0:tq]
        for h in range(1, NSA_HPG):
            p_sum = p_sum + p_c[:, h * tq:(h + 1) * tq]
        p_hi = p_sum.astype(BF16)
        p_lo = (p_sum - p_hi.astype(F32)).astype(BF16)
        imp = _dot(selt_ref[...], p_hi) + _dot(selt_ref[...], p_lo)
        blk = lax.broadcasted_iota(jnp.int32, imp.shape, 0)
        cur = t_q >> (SEL_BLOCK.bit_length() - 1)
        forced = (blk == 0) | (blk == cur) | (blk == cur - 1)
        score = jnp.where(forced, FORCE_SCORE, jnp.where(blk > cur, -FORCE_SCORE, imp))
        beaten = jnp.zeros(imp.shape, F32)
        for i in range(n_sel):
            s_i = score[i:i + 1, :]
            wins = (s_i > score) | ((s_i == score) & (blk > i))
            beaten = beaten + jnp.where(wins, 1.0, 0.0)
        not_sel = jnp.where(beaten < float(SEL_TOP_N), 0.0, 1.0)
        not_sel = jnp.concatenate([not_sel, jnp.zeros((LANES - n_sel, tq), F32)], axis=0).astype(BF16)
        qsel_ref[g, 0:LANES, :] = qt
        qsel_ref[g, LANES:, :] = jnp.concatenate([not_sel] * NSA_HPG, axis=1)

    def score(c, maxes, on_diagonal):
        k0 = pl.multiple_of(c * ATT_CK, ATT_CK)
        k_blk = ks_ref[0, pl.ds(k0, ATT_CK), :]
        out = []
        for g in range(NSA_GROUPS):
            s = _dot(k_blk, qsel_ref[g])
            if on_diagonal:
                kpos = k0 + lax.broadcasted_iota(jnp.int32, s.shape, 0)
                s = jnp.where(kpos <= t_cols, s, MASK_VALUE)
            s_ref[g, pl.ds(k0, ATT_CK), :] = s
            out.append(jnp.maximum(maxes[g], jnp.max(s, axis=0, keepdims=True)))
        return tuple(out)

    last = (t0 + tq - 1) // ATT_CK
    m_init = jnp.full((1, cols), MASK_VALUE, F32)
    maxes = lax.fori_loop(0, last, lambda c, m: score(c, m, False), (m_init,) * NSA_GROUPS)
    maxes = score(last, maxes, True)

    w0 = pl.multiple_of(jnp.maximum(t0 - WINDOW, 0), tq)
    wlen = WINDOW + tq
    wpos = w0 + lax.broadcasted_iota(jnp.int32, (wlen, cols), 0)
    in_window = (wpos <= t_cols) & (wpos > t_cols - WINDOW)
    k_win = kw_ref[0, pl.ds(w0, wlen), :]
    win_maxes = []
    for g in range(NSA_GROUPS):
        s_w = jnp.where(in_window, _dot(k_win, qsel_ref[g, 0:LANES, :]), MASK_VALUE)
        sw_ref[g] = s_w
        win_maxes.append(jnp.max(s_w, axis=0, keepdims=True))

    acc_ref[...] = jnp.zeros(acc_ref.shape, F32)

    def attend(c, carry):
        k0 = pl.multiple_of(c * ATT_CK, ATT_CK)
        for g in range(NSA_GROUPS):
            p = _exp2_bf16(s_ref[g, pl.ds(k0, ATT_CK), :] - maxes[g])
            acc_ref[g] += _dot(vst_ref[0, g, :, pl.ds(k0, ATT_CK)], p)
        return carry

    lax.fori_loop(0, last, attend, 0)
    attend(last, 0)
    acc_win = [_dot(vwt_ref[0, g, :, pl.ds(w0, wlen)], _exp2_bf16(sw_ref[g] - win_maxes[g]))
               for g in range(NSA_GROUPS)]

    gates = gt_ref[0]
    for g in range(NSA_GROUPS):
        acc_s = acc_ref[g]
        o_s = acc_s[0:NSA_DH] * (1.0 / jnp.maximum(acc_s[NSA_DH:NSA_DH + 1], 1e-30))
        o_w = acc_win[g][0:NSA_DH] * (1.0 / jnp.maximum(acc_win[g][NSA_DH:NSA_DH + 1], 1e-30))

        o_c = oc_ref[g]
        heads = []
        for h in range(NSA_HPG):
            c = slice(h * tq, (h + 1) * tq)
            gi = (NSA_HPG * g + h) * 3
            mix = (gates[gi:gi + 1] * o_c[:, c] + gates[gi + 1:gi + 2] * o_s[:, c]
                   + gates[gi + 2:gi + 3] * o_w[:, c])
            heads.append(mix)
        for j in range(NSA_HPG // 2):
            pair = jnp.concatenate([heads[2 * j], heads[2 * j + 1]], axis=0)
            c0 = (NSA_HPG * g + 2 * j) * NSA_DH
            o_ref[0, :, c0:c0 + LANES] = pair.T.astype(BF16)


def _nsa_attention(qt, kc, vct, ks, vst, kw, vwt, gt, selt):
    B, _, _, S = qt.shape
    tq = ATT_TQ
    cols = NSA_HPG * tq
    seq_rows = lambda n: pl.BlockSpec((1, S, n), lambda b, i: (b, 0, 0))
    seq_cols = pl.BlockSpec((1, NSA_GROUPS, PV_ROWS, S), lambda b, i: (b, 0, 0, 0))
    whole = lambda a: pl.BlockSpec((1,) + a.shape[1:], lambda b, i: (b, 0, 0))
    return pl.pallas_call(
        functools.partial(_nsa_kernel, seq_len=S),
        grid=(B, S // tq),
        in_specs=[pl.BlockSpec((1, NSA_HEADS, LANES, tq), lambda b, i: (b, 0, 0, i)),
                  whole(kc), whole(vct), seq_rows(2 * LANES), seq_cols, seq_rows(LANES),
                  seq_cols, pl.BlockSpec((1, GATE_ROWS, tq), lambda b, i: (b, 0, i)),
                  pl.BlockSpec(selt.shape, lambda b, i: (0, 0))],
        out_specs=pl.BlockSpec((1, tq, NSA_WIDTH), lambda b, i: (b, i, 0)),
        out_shape=jax.ShapeDtypeStruct((B, S, NSA_WIDTH), BF16),
        scratch_shapes=[pltpu.VMEM((NSA_GROUPS, 2 * LANES, cols), BF16),
                        pltpu.VMEM((NSA_GROUPS, NSA_DH, cols), F32),
                        pltpu.VMEM((NSA_GROUPS, S, cols), F32),
                        pltpu.VMEM((NSA_GROUPS, WINDOW + tq, cols), F32),
                        pltpu.VMEM((NSA_GROUPS, PV_ROWS, cols), F32)],
        compiler_params=_params("parallel", "parallel"),
        name="nsa_attention",
    )(qt, kc, vct, ks, vst, kw, vwt, gt, selt)


def _norm_mlp_norm(mix_fns, x_ref, y_ref, g1_ref, b1_ref, w1_ref, w2_ref, g2_ref, b2_ref, x1_ref, xb_ref, acc_ref):
    tiles = [slice(i * POST_SUB, (i + 1) * POST_SUB) for i in range(len(mix_fns))]
    n_ff = D_FF // FF_CHUNK

    def norm1(i):
        rows = tiles[i]
        x1_ref[rows] = _layer_norm(ALPHA * x_ref[0, rows] + mix_fns[i](), g1_ref[...], b1_ref[...])
        xb_ref[rows] = x1_ref[rows].astype(BF16)

    def mlp(i, chunks):
        rows = tiles[i]
        for c in chunks:
            cols = slice(c * FF_CHUNK, (c + 1) * FF_CHUNK)
            h = jnp.maximum(_dot(xb_ref[rows], w1_ref[:, cols]), 0.0)
            part = _dot((h * h).astype(BF16), w2_ref[cols, :])
            if c == 0:
                acc_ref[rows] = part
            else:
                acc_ref[rows] += part

    def norm2(i):
        rows = tiles[i]
        y_ref[0, rows] = _layer_norm(ALPHA * x1_ref[rows] + acc_ref[rows], g2_ref[...], b2_ref[...])

    first, second = range(0, n_ff // 2), range(n_ff // 2, n_ff)
    norm1(0)
    for i in range(len(tiles)):
        mlp(i, first)
        if i > 0:
            norm2(i - 1)
        if i + 1 < len(tiles):
            norm1(i + 1)
        mlp(i, second)
    norm2(len(tiles) - 1)


def _post_scratch(tm):
    return [pltpu.VMEM((tm, D_MODEL), F32), pltpu.VMEM((tm, D_MODEL), BF16), pltpu.VMEM((tm, D_MODEL), F32)]


def _post_even_kernel(o_ref, u_ref, x_ref, wo_ref, pw_ref, ps_ref, g1_ref, b1_ref, w1_ref, w2_ref,
                      g2_ref, b2_ref, y_ref, ext_ref, cat_ref, x1_ref, xb_ref, acc_ref):
    tm = u_ref.shape[1]
    i = pl.program_id(1)

    @pl.when(i == 0)
    def _():
        ext_ref[0:POOL_HALO, :] = jnp.zeros((POOL_HALO, POOL_WIDTH), F32)

    ext_ref[POOL_HALO:, :] = u_ref[0]

    def mix(j):
        r0 = j * POST_SUB
        rows = slice(r0, r0 + POST_SUB)
        t = (i * tm + r0 + lax.broadcasted_iota(jnp.int32, (POST_SUB, 1), 0) + 1).astype(F32)
        cat_ref[rows, 0:NSA_WIDTH] = o_ref[0, rows]
        for gi, w in enumerate(POOL_WINDOWS):
            cols = slice(gi * POOL_GROUP_DIM, (gi + 1) * POOL_GROUP_DIM)
            u_g = u_ref[0, rows, cols]
            acc = u_g
            for d in range(1, w):
                acc = acc + ext_ref[POOL_HALO + r0 - d:POOL_HALO + r0 - d + POST_SUB, cols]
            r = acc / jnp.minimum(t, float(w)) - u_g
            y_g = _dot(r.astype(BF16), pw_ref[gi]) * ps_ref[:, cols]
            cat_ref[rows, NSA_WIDTH + gi * POOL_GROUP_DIM:NSA_WIDTH + (gi + 1) * POOL_GROUP_DIM] = y_g.astype(BF16)
        return _dot(cat_ref[rows], wo_ref[...])

    mix_fns = [functools.partial(mix, j) for j in range(tm // POST_SUB)]
    _norm_mlp_norm(mix_fns, x_ref, y_ref, g1_ref, b1_ref, w1_ref, w2_ref, g2_ref, b2_ref,
                   x1_ref, xb_ref, acc_ref)
    ext_ref[0:POOL_HALO, :] = u_ref[0, tm - POOL_HALO:, :]


def _layer_weight_spec(stacked, layer):
    return pl.BlockSpec((None,) + stacked.shape[1:], lambda bb, i: (layer,) + (0,) * (stacked.ndim - 1),
                        pipeline_mode=pl.Buffered(1))


def _post_even(o, u, x, wo, pw, ps, g1, b1, w1, w2, g2, b2, layer):
    B, S, D = x.shape
    tm = POST_TM
    row = lambda n: pl.BlockSpec((1, tm, n), lambda bb, i: (bb, i, 0))
    full = lambda a: pl.BlockSpec(a.shape, lambda bb, i: (0,) * a.ndim)
    return pl.pallas_call(
        _post_even_kernel,
        grid=(B, S // tm),
        in_specs=[row(NSA_WIDTH), row(POOL_WIDTH), row(D), full(wo), full(pw), full(ps), full(g1), full(b1),
                  _layer_weight_spec(w1, layer), _layer_weight_spec(w2, layer), full(g2), full(b2)],
        out_specs=row(D),
        out_shape=jax.ShapeDtypeStruct((B, S, D), F32),
        scratch_shapes=[pltpu.VMEM((POOL_HALO + tm, POOL_WIDTH), F32),
                        pltpu.VMEM((tm, NSA_WIDTH + POOL_WIDTH), BF16)] + _post_scratch(tm),
        compiler_params=_params("parallel", "arbitrary"),
        name="post_even",
    )(o, u, x, wo, pw, ps, g1, b1, w1, w2, g2, b2)


def _proj_odd_kernel(x_ref, w_ref, wa_ref, gw_ref, gb_ref, q_ref, k_ref, v_ref, r_ref, la_ref):
    xb = x_ref[0].astype(BF16)
    a = _dot(xb, wa_ref[...])
    z = _dot(a.astype(BF16), gw_ref[...]) + gb_ref[...]
    log_sig = jnp.minimum(z, 0.0) - jnp.log(1.0 + jnp.exp(-jnp.abs(z)))
    la_ref[0] = log_sig / GLA_TAU
    n = FF_CHUNK
    base = 0
    for ref, width in ((q_ref, GLA_KEY_WIDTH), (k_ref, GLA_KEY_WIDTH), (v_ref, GLA_VAL_WIDTH),
                       (r_ref, GLA_VAL_WIDTH)):
        for c in range(width // n):
            ref[0, :, c * n:(c + 1) * n] = _dot(xb, w_ref[:, base + c * n:base + (c + 1) * n]).astype(ref.dtype)
        base += width


def _proj_odd(x, w, wa, gw, gb):
    B, S, D = x.shape
    tm = PROJ_TM
    row = lambda n: pl.BlockSpec((1, tm, n), lambda b, i: (b, i, 0))
    full = lambda a: pl.BlockSpec(a.shape, lambda b, i: (0,) * a.ndim)
    return pl.pallas_call(
        _proj_odd_kernel,
        grid=(B, S // tm),
        in_specs=[row(D), full(w), full(wa), full(gw), full(gb)],
        out_specs=(row(GLA_KEY_WIDTH), row(GLA_KEY_WIDTH), row(GLA_VAL_WIDTH), row(GLA_VAL_WIDTH),
                   row(GLA_KEY_WIDTH)),
        out_shape=(jax.ShapeDtypeStruct((B, S, GLA_KEY_WIDTH), F32),
                   jax.ShapeDtypeStruct((B, S, GLA_KEY_WIDTH), F32),
                   jax.ShapeDtypeStruct((B, S, GLA_VAL_WIDTH), BF16),
                   jax.ShapeDtypeStruct((B, S, GLA_VAL_WIDTH), F32),
                   jax.ShapeDtypeStruct((B, S, GLA_KEY_WIDTH), F32)),
        compiler_params=_params("parallel", "parallel"),
        name="proj_odd",
    )(x, w, wa, gw, gb)


def _gla_kernel(q_ref, k_ref, la_ref, v_ref, r_ref, ng_ref, o_ref, state_ref):
    C = GLA_CHUNK

    @pl.when(pl.program_id(1) == 0)
    def _():
        state_ref[...] = jnp.zeros(state_ref.shape, F32)

    row = lax.broadcasted_iota(jnp.int32, (C, GLA_DK), 0)
    causal = (lax.broadcasted_iota(jnp.int32, (C, C), 0) >= lax.broadcasted_iota(jnp.int32, (C, C), 1))
    for n in range(q_ref.shape[1] // C):
        rows = slice(n * C, (n + 1) * C)
        for h in range(GLA_HEADS):
            kcols = slice(h * GLA_DK, (h + 1) * GLA_DK)
            vcols = slice(h * GLA_DV, (h + 1) * GLA_DV)
            b = la_ref[0, rows, kcols]
            shift = 1
            while shift < C:
                b = b + jnp.where(row >= shift, pltpu.roll(b, shift, axis=0), 0.0)
                shift *= 2
            b_last = b[C - 1:C, :]
            q_t = (q_ref[0, rows, kcols] * (GLA_DK ** -0.5)) * jnp.exp(b)
            k = k_ref[0, rows, kcols]
            k_t = k * jnp.exp(-b)
            k_u = k * jnp.exp(b_last - b)
            v = v_ref[0, rows, vcols]
            q_tb = q_t.astype(BF16)
            att = jnp.where(causal, _dot_nt(q_tb, k_t.astype(BF16)), 0.0)
            state = state_ref[h]
            o = _dot(att.astype(BF16), v) + _dot_nt(q_tb, state.astype(BF16))
            upd = _dot_tn(v, k_u.astype(BF16))
            state_ref[h] = jnp.exp(b_last) * state + upd
            o = o * lax.rsqrt(jnp.mean(o * o, axis=-1, keepdims=True) + LN_EPS) * ng_ref[...]
            o_ref[0, rows, vcols] = (o * jax.nn.silu(r_ref[0, rows, vcols])).astype(BF16)


def _gla(q, k, la, v, r, ng):
    B, S, _ = q.shape
    row = lambda n: pl.BlockSpec((1, GLA_TILE, n), lambda b, i: (b, i, 0))
    return pl.pallas_call(
        _gla_kernel,
        grid=(B, S // GLA_TILE),
        in_specs=[row(GLA_KEY_WIDTH), row(GLA_KEY_WIDTH), row(GLA_KEY_WIDTH), row(GLA_VAL_WIDTH),
                  row(GLA_VAL_WIDTH), pl.BlockSpec(ng.shape, lambda b, i: (0, 0))],
        out_specs=row(GLA_VAL_WIDTH),
        out_shape=jax.ShapeDtypeStruct((B, S, GLA_VAL_WIDTH), BF16),
        scratch_shapes=[pltpu.VMEM((GLA_HEADS, GLA_DV, GLA_DK), F32)],
        compiler_params=_params("parallel", "arbitrary"),
        name="gla",
    )(q, k, la, v, r, ng)


def _post_odd_kernel(o_ref, x_ref, wo_ref, g1_ref, b1_ref, w1_ref, w2_ref, g2_ref, b2_ref, y_ref,
                     x1_ref, xb_ref, acc_ref):
    def mix(i):
        return _dot(o_ref[0, i * POST_SUB:(i + 1) * POST_SUB], wo_ref[...])

    mix_fns = [functools.partial(mix, i) for i in range(o_ref.shape[1] // POST_SUB)]
    _norm_mlp_norm(mix_fns, x_ref, y_ref, g1_ref, b1_ref, w1_ref, w2_ref, g2_ref, b2_ref,
                   x1_ref, xb_ref, acc_ref)


def _post_odd(o, x, wo, g1, b1, w1, w2, g2, b2, layer):
    B, S, D = x.shape
    tm = POST_TM
    row = lambda n: pl.BlockSpec((1, tm, n), lambda bb, i: (bb, i, 0))
    full = lambda a: pl.BlockSpec(a.shape, lambda bb, i: (0,) * a.ndim)
    return pl.pallas_call(
        _post_odd_kernel,
        grid=(B, S // tm),
        in_specs=[row(GLA_VAL_WIDTH), row(D), full(wo), full(g1), full(b1), _layer_weight_spec(w1, layer),
                  _layer_weight_spec(w2, layer), full(g2), full(b2)],
        out_specs=row(D),
        out_shape=jax.ShapeDtypeStruct((B, S, D), F32),
        scratch_shapes=_post_scratch(tm),
        compiler_params=_params("parallel", "parallel"),
        name="post_odd",
    )(o, x, wo, g1, b1, w1, w2, g2, b2)


def _even_weights(w_in, cmp_pos, cmp_w1, cmp_b1, cmp_w2, pool_w, pool_scale, w_out):
    o1 = NSA_WIDTH
    o2 = o1 + 6 * NSA_KV_WIDTH
    o3 = o2 + 3 * NSA_HEADS
    D = w_in.shape[0]
    wq = (w_in[:, :o1] * (NSA_DH ** -0.5 * LOG2_E)).reshape(D, NSA_GROUPS, NSA_HPG, NSA_DH)
    slots = jnp.zeros((D, NSA_GROUPS, NSA_HPG, NSA_GROUPS, NSA_DH), F32)
    for g in range(NSA_GROUPS):
        slots = slots.at[:, g, :, g, :].set(wq[:, g])
    wq = slots.reshape(D, NSA_HEADS * LANES)
    kv = [w_in[:, o1 + n * NSA_KV_WIDTH:o1 + (n + 1) * NSA_KV_WIDTH] for n in range(6)]
    k_cmp, v_cmp, k_slc, v_slc, k_win, v_win = kv
    wg = jnp.pad(w_in[:, o2:o3], ((0, 0), (0, GATE_ROWS - 3 * NSA_HEADS)))
    wt = jnp.concatenate([wq, v_slc, v_win, wg], axis=1).T.astype(BF16)
    wr = jnp.concatenate([k_cmp, v_cmp, k_slc, k_win, w_in[:, o3:]], axis=1).astype(BF16)
    half = CMP_BLOCK // 2
    eye = jnp.eye(NSA_GROUPS, dtype=F32)
    w1 = cmp_w1.reshape(2, 2, half, NSA_DH, NSA_DH)
    w1 = jnp.einsum('whldo,gk->whlgdko', w1, eye).reshape(2, 2, half * LANES, LANES).astype(BF16)
    pos = cmp_pos.reshape(2, 2, half, 1, NSA_DH)
    pos = jnp.broadcast_to(pos, (2, 2, half, NSA_GROUPS, NSA_DH)).reshape(2, 2, 1, half * LANES)
    b1 = jnp.tile(cmp_b1, (1, NSA_GROUPS)).reshape(2, 1, LANES)
    w2 = jnp.einsum('wdo,gk->wgdko', cmp_w2, eye).reshape(2, LANES, LANES).astype(BF16)
    return dict(wt=wt, wr=wr, pos=pos, w1=w1, b1=b1, w2=w2,
                pw=pool_w.astype(BF16), ps=pool_scale.reshape(1, POOL_WIDTH), wo=w_out.astype(BF16))


def _selection_constants(seq_len):
    n_cmp = (seq_len - CMP_BLOCK) // CMP_STRIDE + 1
    n_sel = seq_len // SEL_BLOCK
    sub = np.arange(n_cmp)[:, None] + np.arange(CMP_BLOCK // CMP_STRIDE)[None, :]
    owner = sub // (SEL_BLOCK // CMP_STRIDE)
    sel_map = (owner[:, :, None] == np.arange(n_sel)[None, None, :]).sum(1).astype(np.float32)
    selt = np.zeros((n_sel, seq_len // CMP_STRIDE), np.float32)
    selt[:, :n_cmp] = sel_map.T
    eb = np.zeros((seq_len, LANES), np.float32)
    eb[np.arange(seq_len), np.arange(seq_len) // SEL_BLOCK] = MASK_VALUE
    return jnp.asarray(selt, BF16), jnp.asarray(eb, BF16)


def _even_layer(x, w, mlp, selt, eb):
    B, S, _ = x.shape
    qt, vst, vwt, gt, kc, vc, ks, kw, u = _proj_even(x, w['wt'], w['wr'], eb)
    kcc, vcct = _compress(kc, vc, w['pos'], w['w1'], w['b1'], w['w2'])
    o = _nsa_attention(qt, kcc, vcct, ks, vst, kw, vwt, gt, selt)
    return _post_even(o, u, x, w['wo'], w['pw'], w['ps'], *mlp)


def _odd_layer(x, w_in, gate_w2, gate_b, norm_g, w_out, mlp):
    wb = w_in.astype(BF16)
    wa = jnp.pad(wb[:, 2 * GLA_KEY_WIDTH + 2 * GLA_VAL_WIDTH:], ((0, 0), (0, LANES - GLA_RANK)))
    gw = jnp.pad(gate_w2.astype(BF16), ((0, LANES - GLA_RANK), (0, 0)))
    q, k, v, r, la = _proj_odd(x, wb, wa, gw, gate_b.reshape(1, GLA_KEY_WIDTH))
    o = _gla(q, k, la, v, r, norm_g.reshape(1, GLA_DV))
    return _post_odd(o, x, w_out.astype(BF16), *mlp)


def kernel(x, a_w_in, a_cmp_pos, a_cmp_w1, a_cmp_b1, a_cmp_w2, a_pool_w, a_pool_scale, a_w_out, c_w_in, c_gate_w2, c_gate_b, c_norm_g, c_w_out, ln1_g, ln1_b, ln2_g, ln2_b, mlp_w1, mlp_w2):
    S = x.shape[1]
    selt, eb = _selection_constants(S)
    w1_all, w2_all = mlp_w1.astype(BF16), mlp_w2.astype(BF16)
    for i in range(DEPTH):
        j = i // 2
        mlp = (ln1_g[i].reshape(1, D_MODEL), ln1_b[i].reshape(1, D_MODEL), w1_all, w2_all,
               ln2_g[i].reshape(1, D_MODEL), ln2_b[i].reshape(1, D_MODEL), i)
        if i % 2 == 0:
            w = _even_weights(a_w_in[j], a_cmp_pos[j], a_cmp_w1[j], a_cmp_b1[j], a_cmp_w2[j],
                              a_pool_w[j], a_pool_scale[j], a_w_out[j])
            x = _even_layer(x, w, mlp, selt, eb)
        else:
            x = _odd_layer(x, c_w_in[j], c_gate_w2[j], c_gate_b[j], c_norm_g[j], c_w_out[j], mlp)
    return x
```

```python
import functools

import numpy as np
import jax
import jax.numpy as jnp
from jax import lax
from jax.experimental import pallas as pl
from jax.experimental.pallas import tpu as pltpu

F32 = jnp.float32
BF16 = jnp.bfloat16

D_MODEL = 1024
DEPTH = 4
NSA_HEADS = 8
NSA_GROUPS = 2
NSA_HPG = NSA_HEADS // NSA_GROUPS
NSA_DH = 64
NSA_WIDTH = NSA_HEADS * NSA_DH
NSA_KV_WIDTH = NSA_GROUPS * NSA_DH
CMP_BLOCK = 32
CMP_STRIDE = 16
SEL_BLOCK = 64
SEL_TOP_N = 16
WINDOW = 512
POOL_WINDOWS = (2, 4, 8, 16)
POOL_GROUP_DIM = 128
POOL_WIDTH = 512
GLA_HEADS = 4
GLA_DK = 128
GLA_DV = 256
GLA_KEY_WIDTH = GLA_HEADS * GLA_DK
GLA_VAL_WIDTH = GLA_HEADS * GLA_DV
GLA_RANK = 16
GLA_TAU = 16.0
GLA_CHUNK = 64
D_FF = 4 * D_MODEL
ALPHA = (2 * DEPTH) ** 0.25
LN_EPS = 1e-5
MASK_VALUE = -1e30
FORCE_SCORE = 1e4
LOG2_E = 1.4426950408889634

LANES = 128
VMEM_LIMIT_BYTES = 56 * 1024 * 1024

PROJ_TM = 1024
POST_TM = 1024
POST_SUB = 512
ATT_TQ = 256
ATT_CK = 512
FF_CHUNK = 512
POOL_HALO = 16
GLA_TILE = 1024
GATE_ROWS = 32
PV_ROWS = 80
PROJ_T_ROWS = NSA_HEADS * LANES + 2 * LANES + GATE_ROWS


def _dot(a, b):
    return jnp.dot(a, b, preferred_element_type=F32)


def _dot_nt(a, b):
    return lax.dot_general(a, b, (((1,), (1,)), ((), ())), preferred_element_type=F32)


def _dot_tn(a, b):
    return lax.dot_general(a, b, (((0,), (0,)), ((), ())), preferred_element_type=F32)


def _params(*semantics):
    return pltpu.CompilerParams(dimension_semantics=semantics, vmem_limit_bytes=VMEM_LIMIT_BYTES)


def _exp2_bf16(d):
    return jnp.exp2(d.astype(BF16))


def _layer_norm(z, g, b):
    mu = jnp.mean(z, axis=-1, keepdims=True)
    zc = z - mu
    var = jnp.mean(zc * zc, axis=-1, keepdims=True)
    return zc * lax.rsqrt(var + LN_EPS) * g + b


def _proj_even_kernel(x_ref, wt_ref, wr_ref, eb_ref,
                      qt_ref, vst_ref, vwt_ref, gt_ref, kc_ref, vc_ref, ks_ref, kw_ref, u_ref, cmp_ref):
    xb = x_ref[0].astype(BF16)
    ht = _dot_nt(wt_ref[...], xb)
    nq = NSA_HEADS * LANES
    for h in range(NSA_HEADS):
        qt_ref[0, h] = ht[h * LANES:(h + 1) * LANES].astype(BF16)
    pad_rows = lax.broadcasted_iota(jnp.int32, (PV_ROWS - NSA_DH, ht.shape[1]), 0)
    ones_then_zeros = jnp.where(pad_rows == 0, 1.0, 0.0)
    for g in range(NSA_GROUPS):
        for ref, base in ((vst_ref, nq), (vwt_ref, nq + LANES)):
            v_g = ht[base + g * NSA_DH:base + (g + 1) * NSA_DH]
            ref[0, g] = jnp.concatenate([v_g, ones_then_zeros], axis=0).astype(BF16)
    gt_ref[0] = jax.nn.sigmoid(ht[nq + 2 * LANES:])
    hr = _dot(xb, wr_ref[...])
    n_rows = hr.shape[0] // CMP_STRIDE
    for which, dst in enumerate((kc_ref, vc_ref)):
        cmp_ref[which] = hr[:, which * LANES:(which + 1) * LANES]
        for l in range(CMP_STRIDE):
            dst[0, :, l * LANES:(l + 1) * LANES] = cmp_ref[which, pl.ds(l, n_rows, stride=CMP_STRIDE), :]
    ks_ref[0, :, :LANES] = hr[:, 2 * LANES:3 * LANES].astype(BF16)
    ks_ref[0, :, LANES:] = eb_ref[...]
    kw_ref[0] = hr[:, 3 * LANES:4 * LANES].astype(BF16)
    u_ref[0] = hr[:, 4 * LANES:]


def _proj_even(x, wt, wr, eb):
    B, S, D = x.shape
    tm = PROJ_TM
    row = lambda n: pl.BlockSpec((1, tm, n), lambda b, i: (b, i, 0))
    col = lambda n: pl.BlockSpec((1, n, tm), lambda b, i: (b, 0, i))
    full = lambda a: pl.BlockSpec(a.shape, lambda b, i: (0,) * a.ndim)
    out_shape = (
        jax.ShapeDtypeStruct((B, NSA_HEADS, LANES, S), BF16),
        jax.ShapeDtypeStruct((B, NSA_GROUPS, PV_ROWS, S), BF16),
        jax.ShapeDtypeStruct((B, NSA_GROUPS, PV_ROWS, S), BF16),
        jax.ShapeDtypeStruct((B, GATE_ROWS, S), F32),
        jax.ShapeDtypeStruct((B, S // CMP_STRIDE, CMP_STRIDE * LANES), F32),
        jax.ShapeDtypeStruct((B, S // CMP_STRIDE, CMP_STRIDE * LANES), F32),
        jax.ShapeDtypeStruct((B, S, 2 * LANES), BF16),
        jax.ShapeDtypeStruct((B, S, LANES), BF16),
        jax.ShapeDtypeStruct((B, S, POOL_WIDTH), F32),
    )
    out_specs = (
        pl.BlockSpec((1, NSA_HEADS, LANES, tm), lambda b, i: (b, 0, 0, i)),
        pl.BlockSpec((1, NSA_GROUPS, PV_ROWS, tm), lambda b, i: (b, 0, 0, i)),
        pl.BlockSpec((1, NSA_GROUPS, PV_ROWS, tm), lambda b, i: (b, 0, 0, i)), col(GATE_ROWS),
        pl.BlockSpec((1, tm // CMP_STRIDE, CMP_STRIDE * LANES), lambda b, i: (b, i, 0)),
        pl.BlockSpec((1, tm // CMP_STRIDE, CMP_STRIDE * LANES), lambda b, i: (b, i, 0)),
        row(2 * LANES), row(LANES), row(POOL_WIDTH),
    )
    return pl.pallas_call(
        _proj_even_kernel,
        grid=(B, S // tm),
        in_specs=[row(D), full(wt), full(wr), pl.BlockSpec((tm, LANES), lambda b, i: (i, 0))],
        out_specs=out_specs,
        out_shape=out_shape,
        scratch_shapes=[pltpu.VMEM((2, tm, LANES), F32)],
        compiler_params=_params("parallel", "parallel"),
        name="proj_even",
    )(x, wt, wr, eb)


def _gelu_tanh(x):
    return 0.5 * x * (1.0 + jnp.tanh(0.7978845608028654 * (x + 0.044715 * (x * x * x))))


def _compress_kernel(kc_ref, vc_ref, pos_ref, w1_ref, b1_ref, w2_ref, okc_ref, ovct_ref):
    for which, src in enumerate((kc_ref, vc_ref)):
        xr = src[0]
        nr = xr.shape[0]
        lo = _dot((xr + pos_ref[which, 0]).astype(BF16), w1_ref[which, 0])
        hi = _dot((xr + pos_ref[which, 1]).astype(BF16), w1_ref[which, 1])
        hi_next = pltpu.roll(hi, nr - 1, axis=0)
        act = _gelu_tanh(lo + hi_next + b1_ref[which])
        out = _dot(act.astype(BF16), w2_ref[which])
        rows = lax.broadcasted_iota(jnp.int32, out.shape, 0)
        out = jnp.where(rows < nr - 1, out, 0.0)
        if which == 0:
            okc_ref[0] = out.astype(BF16)
        else:
            ovct_ref[0] = out.T.astype(BF16)


def _compress(kc, vc, pos, w1, b1, w2):
    B, NR, W = kc.shape
    blk = pl.BlockSpec((1, NR, W), lambda b: (b, 0, 0))
    full = lambda a: pl.BlockSpec(a.shape, lambda b: (0,) * a.ndim)
    return pl.pallas_call(
        _compress_kernel,
        grid=(B,),
        in_specs=[blk, blk, full(pos), full(w1), full(b1), full(w2)],
        out_specs=(pl.BlockSpec((1, NR, LANES), lambda b: (b, 0, 0)),
                   pl.BlockSpec((1, LANES, NR), lambda b: (b, 0, 0))),
        out_shape=(jax.ShapeDtypeStruct((B, NR, LANES), BF16),
                   jax.ShapeDtypeStruct((B, LANES, NR), BF16)),
        compiler_params=_params("parallel"),
        name="compress",
    )(kc, vc, pos, w1, b1, w2)


def _nsa_kernel(qt_ref, kc_ref, vct_ref, ks_ref, vst_ref, kw_ref, vwt_ref, gt_ref, selt_ref, o_ref,
                qsel_ref, oc_ref, s_ref, sw_ref, acc_ref, *, seq_len):
    tq = ATT_TQ
    cols = NSA_HPG * tq
    n_sel = seq_len // SEL_BLOCK
    t0 = pl.program_id(1) * tq
    t_q = t0 + lax.broadcasted_iota(jnp.int32, (1, tq), 1)
    t_cols = jnp.concatenate([t_q] * NSA_HPG, axis=1)

    for g in range(NSA_GROUPS):
        qt = jnp.concatenate([qt_ref[0, NSA_HPG * g + h] for h in range(NSA_HPG)], axis=1)

        s_c = _dot(kc_ref[0], qt)
        c_end = lax.broadcasted_iota(jnp.int32, s_c.shape, 0) * CMP_STRIDE + (CMP_BLOCK - 1)
        valid_c = c_end <= t_cols
        s_c = jnp.where(valid_c, s_c, MASK_VALUE)
        e_c = jnp.where(valid_c, jnp.exp2(s_c - jnp.max(s_c, axis=0, keepdims=True)), 0.0)
        p_c = e_c * (1.0 / jnp.maximum(jnp.sum(e_c, axis=0, keepdims=True), 1e-30))
        oc_ref[g] = _dot(vct_ref[0], p_c.astype(BF16))[g * NSA_DH:(g + 1) * NSA_DH]

        p_sum = p_c[:, 0:tq]
        for h in range(1, NSA_HPG):
            p_sum = p_sum + p_c[:, h * tq:(h + 1) * tq]
        p_hi = p_sum.astype(BF16)
        p_lo = (p_sum - p_hi.astype(F32)).astype(BF16)
        imp = _dot(selt_ref[...], p_hi) + _dot(selt_ref[...], p_lo)
        blk = lax.broadcasted_iota(jnp.int32, imp.shape, 0)
        cur = t_q >> (SEL_BLOCK.bit_length() - 1)
        forced = (blk == 0) | (blk == cur) | (blk == cur - 1)
        score = jnp.where(forced, FORCE_SCORE, jnp.where(blk > cur, -FORCE_SCORE, imp))
        beaten = jnp.zeros(imp.shape, F32)
        for i in range(n_sel):
            s_i = score[i:i + 1, :]
            wins = (s_i > score) | ((s_i == score) & (blk > i))
            beaten = beaten + jnp.where(wins, 1.0, 0.0)
        not_sel = jnp.where(beaten < float(SEL_TOP_N), 0.0, 1.0)
        not_sel = jnp.concatenate([not_sel, jnp.zeros((LANES - n_sel, tq), F32)], axis=0).astype(BF16)
        qsel_ref[g, 0:LANES, :] = qt
        qsel_ref[g, LANES:, :] = jnp.concatenate([not_sel] * NSA_HPG, axis=1)

    def score(c, maxes, on_diagonal):
        k0 = pl.multiple_of(c * ATT_CK, ATT_CK)
        k_blk = ks_ref[0, pl.ds(k0, ATT_CK), :]
        out = []
        for g in range(NSA_GROUPS):
            s = _dot(k_blk, qsel_ref[g])
            if on_diagonal:
                kpos = k0 + lax.broadcasted_iota(jnp.int32, s.shape, 0)
                s = jnp.where(kpos <= t_cols, s, MASK_VALUE)
            s_ref[g, pl.ds(k0, ATT_CK), :] = s
            out.append(jnp.maximum(maxes[g], jnp.max(s, axis=0, keepdims=True)))
        return tuple(out)

    last = (t0 + tq - 1) // ATT_CK
    m_init = jnp.full((1, cols), MASK_VALUE, F32)
    maxes = lax.fori_loop(0, last, lambda c, m: score(c, m, False), (m_init,) * NSA_GROUPS)
    maxes = score(last, maxes, True)

    w0 = pl.multiple_of(jnp.maximum(t0 - WINDOW, 0), tq)
    wlen = WINDOW + tq
    wpos = w0 + lax.broadcasted_iota(jnp.int32, (wlen, cols), 0)
    in_window = (wpos <= t_cols) & (wpos > t_cols - WINDOW)
    k_win = kw_ref[0, pl.ds(w0, wlen), :]
    win_maxes = []
    for g in range(NSA_GROUPS):
        s_w = jnp.where(in_window, _dot(k_win, qsel_ref[g, 0:LANES, :]), MASK_VALUE)
        sw_ref[g] = s_w
        win_maxes.append(jnp.max(s_w, axis=0, keepdims=True))

    acc_ref[...] = jnp.zeros(acc_ref.shape, F32)

    def attend(c, carry):
        k0 = pl.multiple_of(c * ATT_CK, ATT_CK)
        for g in range(NSA_GROUPS):
            p = _exp2_bf16(s_ref[g, pl.ds(k0, ATT_CK), :] - maxes[g])
            acc_ref[g] += _dot(vst_ref[0, g, :, pl.ds(k0, ATT_CK)], p)
        return carry

    lax.fori_loop(0, last, attend, 0)
    attend(last, 0)
    acc_win = [_dot(vwt_ref[0, g, :, pl.ds(w0, wlen)], _exp2_bf16(sw_ref[g] - win_maxes[g]))
               for g in range(NSA_GROUPS)]

    gates = gt_ref[0]
    for g in range(NSA_GROUPS):
        acc_s = acc_ref[g]
        o_s = acc_s[0:NSA_DH] * (1.0 / jnp.maximum(acc_s[NSA_DH:NSA_DH + 1], 1e-30))
        o_w = acc_win[g][0:NSA_DH] * (1.0 / jnp.maximum(acc_win[g][NSA_DH:NSA_DH + 1], 1e-30))

        o_c = oc_ref[g]
        heads = []
        for h in range(NSA_HPG):
            c = slice(h * tq, (h + 1) * tq)
            gi = (NSA_HPG * g + h) * 3
            mix = (gates[gi:gi + 1] * o_c[:, c] + gates[gi + 1:gi + 2] * o_s[:, c]
                   + gates[gi + 2:gi + 3] * o_w[:, c])
            heads.append(mix)
        for j in range(NSA_HPG // 2):
            pair = jnp.concatenate([heads[2 * j], heads[2 * j + 1]], axis=0)
            c0 = (NSA_HPG * g + 2 * j) * NSA_DH
            o_ref[0, :, c0:c0 + LANES] = pair.T.astype(BF16)


def _nsa_attention(qt, kc, vct, ks, vst, kw, vwt, gt, selt):
    B, _, _, S = qt.shape
    tq = ATT_TQ
    cols = NSA_HPG * tq
    seq_rows = lambda n: pl.BlockSpec((1, S, n), lambda b, i: (b, 0, 0))
    seq_cols = pl.BlockSpec((1, NSA_GROUPS, PV_ROWS, S), lambda b, i: (b, 0, 0, 0))
    whole = lambda a: pl.BlockSpec((1,) + a.shape[1:], lambda b, i: (b, 0, 0))
    return pl.pallas_call(
        functools.partial(_nsa_kernel, seq_len=S),
        grid=(B, S // tq),
        in_specs=[pl.BlockSpec((1, NSA_HEADS, LANES, tq), lambda b, i: (b, 0, 0, i)),
                  whole(kc), whole(vct), seq_rows(2 * LANES), seq_cols, seq_rows(LANES),
                  seq_cols, pl.BlockSpec((1, GATE_ROWS, tq), lambda b, i: (b, 0, i)),
                  pl.BlockSpec(selt.shape, lambda b, i: (0, 0))],
        out_specs=pl.BlockSpec((1, tq, NSA_WIDTH), lambda b, i: (b, i, 0)),
        out_shape=jax.ShapeDtypeStruct((B, S, NSA_WIDTH), BF16),
        scratch_shapes=[pltpu.VMEM((NSA_GROUPS, 2 * LANES, cols), BF16),
                        pltpu.VMEM((NSA_GROUPS, NSA_DH, cols), F32),
                        pltpu.VMEM((NSA_GROUPS, S, cols), F32),
                        pltpu.VMEM((NSA_GROUPS, WINDOW + tq, cols), F32),
                        pltpu.VMEM((NSA_GROUPS, PV_ROWS, cols), F32)],
        compiler_params=_params("parallel", "parallel"),
        name="nsa_attention",
    )(qt, kc, vct, ks, vst, kw, vwt, gt, selt)


def _norm_mlp_norm(mix_fns, x_ref, y_ref, g1_ref, b1_ref, w1_ref, w2_ref, g2_ref, b2_ref, x1_ref, xb_ref, acc_ref):
    tiles = [slice(i * POST_SUB, (i + 1) * POST_SUB) for i in range(len(mix_fns))]
    n_ff = D_FF // FF_CHUNK

    def norm1(i):
        rows = tiles[i]
        x1_ref[rows] = _layer_norm(ALPHA * x_ref[0, rows] + mix_fns[i](), g1_ref[...], b1_ref[...])
        xb_ref[rows] = x1_ref[rows].astype(BF16)

    def mlp(i, chunks):
        rows = tiles[i]
        for c in chunks:
            cols = slice(c * FF_CHUNK, (c + 1) * FF_CHUNK)
            h = jnp.maximum(_dot(xb_ref[rows], w1_ref[:, cols]), 0.0)
            part = _dot((h * h).astype(BF16), w2_ref[cols, :])
            if c == 0:
                acc_ref[rows] = part
            else:
                acc_ref[rows] += part

    def norm2(i):
        rows = tiles[i]
        y_ref[0, rows] = _layer_norm(ALPHA * x1_ref[rows] + acc_ref[rows], g2_ref[...], b2_ref[...])

    first, second = range(0, n_ff // 2), range(n_ff // 2, n_ff)
    norm1(0)
    for i in range(len(tiles)):
        mlp(i, first)
        if i > 0:
            norm2(i - 1)
        if i + 1 < len(tiles):
            norm1(i + 1)
        mlp(i, second)
    norm2(len(tiles) - 1)


def _post_scratch(tm):
    return [pltpu.VMEM((tm, D_MODEL), F32), pltpu.VMEM((tm, D_MODEL), BF16), pltpu.VMEM((tm, D_MODEL), F32)]


def _post_even_kernel(o_ref, u_ref, x_ref, wo_ref, pw_ref, ps_ref, g1_ref, b1_ref, w1_ref, w2_ref,
                      g2_ref, b2_ref, y_ref, ext_ref, cat_ref, x1_ref, xb_ref, acc_ref):
    tm = u_ref.shape[1]
    i = pl.program_id(1)

    @pl.when(i == 0)
    def _():
        ext_ref[0:POOL_HALO, :] = jnp.zeros((POOL_HALO, POOL_WIDTH), F32)

    ext_ref[POOL_HALO:, :] = u_ref[0]

    def mix(j):
        r0 = j * POST_SUB
        rows = slice(r0, r0 + POST_SUB)
        t = (i * tm + r0 + lax.broadcasted_iota(jnp.int32, (POST_SUB, 1), 0) + 1).astype(F32)
        cat_ref[rows, 0:NSA_WIDTH] = o_ref[0, rows]
        for gi, w in enumerate(POOL_WINDOWS):
            cols = slice(gi * POOL_GROUP_DIM, (gi + 1) * POOL_GROUP_DIM)
            u_g = u_ref[0, rows, cols]
            acc = u_g
            for d in range(1, w):
                acc = acc + ext_ref[POOL_HALO + r0 - d:POOL_HALO + r0 - d + POST_SUB, cols]
            r = acc / jnp.minimum(t, float(w)) - u_g
            y_g = _dot(r.astype(BF16), pw_ref[gi]) * ps_ref[:, cols]
            cat_ref[rows, NSA_WIDTH + gi * POOL_GROUP_DIM:NSA_WIDTH + (gi + 1) * POOL_GROUP_DIM] = y_g.astype(BF16)
        return _dot(cat_ref[rows], wo_ref[...])

    mix_fns = [functools.partial(mix, j) for j in range(tm // POST_SUB)]
    _norm_mlp_norm(mix_fns, x_ref, y_ref, g1_ref, b1_ref, w1_ref, w2_ref, g2_ref, b2_ref,
                   x1_ref, xb_ref, acc_ref)
    ext_ref[0:POOL_HALO, :] = u_ref[0, tm - POOL_HALO:, :]


def _layer_weight_spec(stacked, layer):
    return pl.BlockSpec((None,) + stacked.shape[1:], lambda bb, i: (layer,) + (0,) * (stacked.ndim - 1),
                        pipeline_mode=pl.Buffered(1))


def _post_even(o, u, x, wo, pw, ps, g1, b1, w1, w2, g2, b2, layer):
    B, S, D = x.shape
    tm = POST_TM
    row = lambda n: pl.BlockSpec((1, tm, n), lambda bb, i: (bb, i, 0))
    full = lambda a: pl.BlockSpec(a.shape, lambda bb, i: (0,) * a.ndim)
    return pl.pallas_call(
        _post_even_kernel,
        grid=(B, S // tm),
        in_specs=[row(NSA_WIDTH), row(POOL_WIDTH), row(D), full(wo), full(pw), full(ps), full(g1), full(b1),
                  _layer_weight_spec(w1, layer), _layer_weight_spec(w2, layer), full(g2), full(b2)],
        out_specs=row(D),
        out_shape=jax.ShapeDtypeStruct((B, S, D), F32),
        scratch_shapes=[pltpu.VMEM((POOL_HALO + tm, POOL_WIDTH), F32),
                        pltpu.VMEM((tm, NSA_WIDTH + POOL_WIDTH), BF16)] + _post_scratch(tm),
        compiler_params=_params("parallel", "arbitrary"),
        name="post_even",
    )(o, u, x, wo, pw, ps, g1, b1, w1, w2, g2, b2)


def _proj_odd_kernel(x_ref, w_ref, wa_ref, gw_ref, gb_ref, q_ref, k_ref, v_ref, r_ref, la_ref):
    xb = x_ref[0].astype(BF16)
    a = _dot(xb, wa_ref[...])
    z = _dot(a.astype(BF16), gw_ref[...]) + gb_ref[...]
    log_sig = jnp.minimum(z, 0.0) - jnp.log(1.0 + jnp.exp(-jnp.abs(z)))
    la_ref[0] = log_sig / GLA_TAU
    n = FF_CHUNK
    base = 0
    for ref, width in ((q_ref, GLA_KEY_WIDTH), (k_ref, GLA_KEY_WIDTH), (v_ref, GLA_VAL_WIDTH),
                       (r_ref, GLA_VAL_WIDTH)):
        for c in range(width // n):
            ref[0, :, c * n:(c + 1) * n] = _dot(xb, w_ref[:, base + c * n:base + (c + 1) * n]).astype(ref.dtype)
        base += width


def _proj_odd(x, w, wa, gw, gb):
    B, S, D = x.shape
    tm = PROJ_TM
    row = lambda n: pl.BlockSpec((1, tm, n), lambda b, i: (b, i, 0))
    full = lambda a: pl.BlockSpec(a.shape, lambda b, i: (0,) * a.ndim)
    return pl.pallas_call(
        _proj_odd_kernel,
        grid=(B, S // tm),
        in_specs=[row(D), full(w), full(wa), full(gw), full(gb)],
        out_specs=(row(GLA_KEY_WIDTH), row(GLA_KEY_WIDTH), row(GLA_VAL_WIDTH), row(GLA_VAL_WIDTH),
                   row(GLA_KEY_WIDTH)),
        out_shape=(jax.ShapeDtypeStruct((B, S, GLA_KEY_WIDTH), F32),
                   jax.ShapeDtypeStruct((B, S, GLA_KEY_WIDTH), F32),
                   jax.ShapeDtypeStruct((B, S, GLA_VAL_WIDTH), BF16),
                   jax.ShapeDtypeStruct((B, S, GLA_VAL_WIDTH), F32),
                   jax.ShapeDtypeStruct((B, S, GLA_KEY_WIDTH), F32)),
        compiler_params=_params("parallel", "parallel"),
        name="proj_odd",
    )(x, w, wa, gw, gb)


def _gla_kernel(q_ref, k_ref, la_ref, v_ref, r_ref, ng_ref, o_ref, state_ref):
    C = GLA_CHUNK

    @pl.when(pl.program_id(1) == 0)
    def _():
        state_ref[...] = jnp.zeros(state_ref.shape, F32)

    row = lax.broadcasted_iota(jnp.int32, (C, GLA_DK), 0)
    causal = (lax.broadcasted_iota(jnp.int32, (C, C), 0) >= lax.broadcasted_iota(jnp.int32, (C, C), 1))
    for n in range(q_ref.shape[1] // C):
        rows = slice(n * C, (n + 1) * C)
        for h in range(GLA_HEADS):
            kcols = slice(h * GLA_DK, (h + 1) * GLA_DK)
            vcols = slice(h * GLA_DV, (h + 1) * GLA_DV)
            b = la_ref[0, rows, kcols]
            shift = 1
            while shift < C:
                b = b + jnp.where(row >= shift, pltpu.roll(b, shift, axis=0), 0.0)
                shift *= 2
            b_last = b[C - 1:C, :]
            q_t = (q_ref[0, rows, kcols] * (GLA_DK ** -0.5)) * jnp.exp(b)
            k = k_ref[0, rows, kcols]
            k_t = k * jnp.exp(-b)
            k_u = k * jnp.exp(b_last - b)
            v = v_ref[0, rows, vcols]
            q_tb = q_t.astype(BF16)
            att = jnp.where(causal, _dot_nt(q_tb, k_t.astype(BF16)), 0.0)
            state = state_ref[h]
            o = _dot(att.astype(BF16), v) + _dot_nt(q_tb, state.astype(BF16))
            upd = _dot_tn(v, k_u.astype(BF16))
            state_ref[h] = jnp.exp(b_last) * state + upd
            o = o * lax.rsqrt(jnp.mean(o * o, axis=-1, keepdims=True) + LN_EPS) * ng_ref[...]
            o_ref[0, rows, vcols] = (o * jax.nn.silu(r_ref[0, rows, vcols])).astype(BF16)


def _gla(q, k, la, v, r, ng):
    B, S, _ = q.shape
    row = lambda n: pl.BlockSpec((1, GLA_TILE, n), lambda b, i: (b, i, 0))
    return pl.pallas_call(
        _gla_kernel,
        grid=(B, S // GLA_TILE),
        in_specs=[row(GLA_KEY_WIDTH), row(GLA_KEY_WIDTH), row(GLA_KEY_WIDTH), row(GLA_VAL_WIDTH),
                  row(GLA_VAL_WIDTH), pl.BlockSpec(ng.shape, lambda b, i: (0, 0))],
        out_specs=row(GLA_VAL_WIDTH),
        out_shape=jax.ShapeDtypeStruct((B, S, GLA_VAL_WIDTH), BF16),
        scratch_shapes=[pltpu.VMEM((GLA_HEADS, GLA_DV, GLA_DK), F32)],
        compiler_params=_params("parallel", "arbitrary"),
        name="gla",
    )(q, k, la, v, r, ng)


def _post_odd_kernel(o_ref, x_ref, wo_ref, g1_ref, b1_ref, w1_ref, w2_ref, g2_ref, b2_ref, y_ref,
                     x1_ref, xb_ref, acc_ref):
    def mix(i):
        return _dot(o_ref[0, i * POST_SUB:(i + 1) * POST_SUB], wo_ref[...])

    mix_fns = [functools.partial(mix, i) for i in range(o_ref.shape[1] // POST_SUB)]
    _norm_mlp_norm(mix_fns, x_ref, y_ref, g1_ref, b1_ref, w1_ref, w2_ref, g2_ref, b2_ref,
                   x1_ref, xb_ref, acc_ref)


def _post_odd(o, x, wo, g1, b1, w1, w2, g2, b2, layer):
    B, S, D = x.shape
    tm = POST_TM
    row = lambda n: pl.BlockSpec((1, tm, n), lambda bb, i: (bb, i, 0))
    full = lambda a: pl.BlockSpec(a.shape, lambda bb, i: (0,) * a.ndim)
    return pl.pallas_call(
        _post_odd_kernel,
        grid=(B, S // tm),
        in_specs=[row(GLA_VAL_WIDTH), row(D), full(wo), full(g1), full(b1), _layer_weight_spec(w1, layer),
                  _layer_weight_spec(w2, layer), full(g2), full(b2)],
        out_specs=row(D),
        out_shape=jax.ShapeDtypeStruct((B, S, D), F32),
        scratch_shapes=_post_scratch(tm),
        compiler_params=_params("parallel", "parallel"),
        name="post_odd",
    )(o, x, wo, g1, b1, w1, w2, g2, b2)


def _even_weights(w_in, cmp_pos, cmp_w1, cmp_b1, cmp_w2, pool_w, pool_scale, w_out):
    o1 = NSA_WIDTH
    o2 = o1 + 6 * NSA_KV_WIDTH
    o3 = o2 + 3 * NSA_HEADS
    D = w_in.shape[0]
    wq = (w_in[:, :o1] * (NSA_DH ** -0.5 * LOG2_E)).reshape(D, NSA_GROUPS, NSA_HPG, NSA_DH)
    slots = jnp.zeros((D, NSA_GROUPS, NSA_HPG, NSA_GROUPS, NSA_DH), F32)
    for g in range(NSA_GROUPS):
        slots = slots.at[:, g, :, g, :].set(wq[:, g])
    wq = slots.reshape(D, NSA_HEADS * LANES)
    kv = [w_in[:, o1 + n * NSA_KV_WIDTH:o1 + (n + 1) * NSA_KV_WIDTH] for n in range(6)]
    k_cmp, v_cmp, k_slc, v_slc, k_win, v_win = kv
    wg = jnp.pad(w_in[:, o2:o3], ((0, 0), (0, GATE_ROWS - 3 * NSA_HEADS)))
    wt = jnp.concatenate([wq, v_slc, v_win, wg], axis=1).T.astype(BF16)
    wr = jnp.concatenate([k_cmp, v_cmp, k_slc, k_win, w_in[:, o3:]], axis=1).astype(BF16)
    half = CMP_BLOCK // 2
    eye = jnp.eye(NSA_GROUPS, dtype=F32)
    w1 = cmp_w1.reshape(2, 2, half, NSA_DH, NSA_DH)
    w1 = jnp.einsum('whldo,gk->whlgdko', w1, eye).reshape(2, 2, half * LANES, LANES).astype(BF16)
    pos = cmp_pos.reshape(2, 2, half, 1, NSA_DH)
    pos = jnp.broadcast_to(pos, (2, 2, half, NSA_GROUPS, NSA_DH)).reshape(2, 2, 1, half * LANES)
    b1 = jnp.tile(cmp_b1, (1, NSA_GROUPS)).reshape(2, 1, LANES)
    w2 = jnp.einsum('wdo,gk->wgdko', cmp_w2, eye).reshape(2, LANES, LANES).astype(BF16)
    return dict(wt=wt, wr=wr, pos=pos, w1=w1, b1=b1, w2=w2,
                pw=pool_w.astype(BF16), ps=pool_scale.reshape(1, POOL_WIDTH), wo=w_out.astype(BF16))


def _selection_constants(seq_len):
    n_cmp = (seq_len - CMP_BLOCK) // CMP_STRIDE + 1
    n_sel = seq_len // SEL_BLOCK
    sub = np.arange(n_cmp)[:, None] + np.arange(CMP_BLOCK // CMP_STRIDE)[None, :]
    owner = sub // (SEL_BLOCK // CMP_STRIDE)
    sel_map = (owner[:, :, None] == np.arange(n_sel)[None, None, :]).sum(1).astype(np.float32)
    selt = np.zeros((n_sel, seq_len // CMP_STRIDE), np.float32)
    selt[:, :n_cmp] = sel_map.T
    eb = np.zeros((seq_len, LANES), np.float32)
    eb[np.arange(seq_len), np.arange(seq_len) // SEL_BLOCK] = MASK_VALUE
    return jnp.asarray(selt, BF16), jnp.asarray(eb, BF16)


def _even_layer(x, w, mlp, selt, eb):
    B, S, _ = x.shape
    qt, vst, vwt, gt, kc, vc, ks, kw, u = _proj_even(x, w['wt'], w['wr'], eb)
    kcc, vcct = _compress(kc, vc, w['pos'], w['w1'], w['b1'], w['w2'])
    o = _nsa_attention(qt, kcc, vcct, ks, vst, kw, vwt, gt, selt)
    return _post_even(o, u, x, w['wo'], w['pw'], w['ps'], *mlp)


def _odd_layer(x, w_in, gate_w2, gate_b, norm_g, w_out, mlp):
    wb = w_in.astype(BF16)
    wa = jnp.pad(wb[:, 2 * GLA_KEY_WIDTH + 2 * GLA_VAL_WIDTH:], ((0, 0), (0, LANES - GLA_RANK)))
    gw = jnp.pad(gate_w2.astype(BF16), ((0, LANES - GLA_RANK), (0, 0)))
    q, k, v, r, la = _proj_odd(x, wb, wa, gw, gate_b.reshape(1, GLA_KEY_WIDTH))
    o = _gla(q, k, la, v, r, norm_g.reshape(1, GLA_DV))
    return _post_odd(o, x, w_out.astype(BF16), *mlp)


def kernel(x, a_w_in, a_cmp_pos, a_cmp_w1, a_cmp_b1, a_cmp_w2, a_pool_w, a_pool_scale, a_w_out, c_w_in, c_gate_w2, c_gate_b, c_norm_g, c_w_out, ln1_g, ln1_b, ln2_g, ln2_b, mlp_w1, mlp_w2):
    S = x.shape[1]
    selt, eb = _selection_constants(S)
    w1_all, w2_all = mlp_w1.astype(BF16), mlp_w2.astype(BF16)
    for i in range(DEPTH):
        j = i // 2
        mlp = (ln1_g[i].reshape(1, D_MODEL), ln1_b[i].reshape(1, D_MODEL), w1_all, w2_all,
               ln2_g[i].reshape(1, D_MODEL), ln2_b[i].reshape(1, D_MODEL), i)
        if i % 2 == 0:
            w = _even_weights(a_w_in[j], a_cmp_pos[j], a_cmp_w1[j], a_cmp_b1[j], a_cmp_w2[j],
                              a_pool_w[j], a_pool_scale[j], a_w_out[j])
            x = _even_layer(x, w, mlp, selt, eb)
        else:
            x = _odd_layer(x, c_w_in[j], c_gate_w2[j], c_gate_b[j], c_norm_g[j], c_w_out[j], mlp)
    return x
```
